```python
import math
import jax
import jax.numpy as jnp
from jax import lax
import numpy as np

D_MODEL = 1024
BATCH = 8
SEQ = 2048
DEPTH = 2
DEC_BATCH = 128
DEC_SEQ = 4
PAST_LEN = 16384
PAGE_SIZE = 128

SSM_D_INNER = 1536
SSM_HEAD_DIM = 64
SSM_HEADS = SSM_D_INNER // SSM_HEAD_DIM
SSM_GROUPS = 4
SSM_HEADS_PER_GROUP = SSM_HEADS // SSM_GROUPS
SSM_STATE = 128
SSM_CONV = 4
SSM_CHUNK = 128
SSM_CONV_DIM = SSM_D_INNER + 2 * SSM_GROUPS * SSM_STATE
GM_WIDTH = 512
GM_GROUPS = 4
GM_GROUP_DIM = GM_WIDTH // GM_GROUPS
GM_CHUNK = 128
IN_DIM = SSM_D_INNER + SSM_CONV_DIM + SSM_HEADS + 2 * GM_WIDTH
MEM_LEN = 256
MEM_HEADS = 4
MEM_HEAD_DIM = D_MODEL // MEM_HEADS
MOE_GROUPS = 4
MOE_EXPERTS_PER_GROUP = 4
MOE_EXPERTS = MOE_GROUPS * MOE_EXPERTS_PER_GROUP
MOE_TOP_K = 2
MOE_D_FF = 512
MOE_BLOCK = 128
NORM_EPS = 1e-6

kernel_name = 'hybrid_ssd_gmlp_hmoe_decode_step'


def rmsnorm(x, g):
    xf = x.astype(jnp.float32)
    xf = xf * lax.rsqrt(jnp.mean(xf * xf, axis=-1, keepdims=True) + NORM_EPS)
    return (xf * g.astype(jnp.float32)).astype(x.dtype)


def layernorm(x, g, b):
    xf = x.astype(jnp.float32)
    xc = xf - jnp.mean(xf, axis=-1, keepdims=True)
    var = jnp.mean(xc * xc, axis=-1, keepdims=True)
    return (xc * lax.rsqrt(var + NORM_EPS) * g.astype(jnp.float32) + b.astype(jnp.float32)).astype(x.dtype)


def gated_rmsnorm(y, z, g):
    yz = y.astype(jnp.float32) * jax.nn.silu(z.astype(jnp.float32))
    shp = yz.shape
    yz = yz.reshape(shp[:-1] + (SSM_GROUPS, SSM_D_INNER // SSM_GROUPS))
    yz = yz * lax.rsqrt(jnp.mean(yz * yz, axis=-1, keepdims=True) + NORM_EPS)
    return (yz.reshape(shp) * g.astype(jnp.float32)).astype(z.dtype)


def causal_depthwise_conv(xbc, prev, w, b):
    xpad = jnp.concatenate([prev.astype(xbc.dtype), xbc], axis=1)
    out = lax.conv_general_dilated(
        xpad, w[:, None, :].astype(xbc.dtype), window_strides=(1,), padding='VALID',
        dimension_numbers=('NWC', 'WIO', 'NWC'), feature_group_count=SSM_CONV_DIM)
    return out + b.astype(xbc.dtype), xpad[:, xpad.shape[1] - (SSM_CONV - 1):]


def ssd_scan(x, dt, a, bmat, cmat, h0):
    bsz, L = x.shape[:2]
    cs = SSM_CHUNK if L % SSM_CHUNK == 0 else L
    nc = L // cs
    G, R, P, N = SSM_GROUPS, SSM_HEADS_PER_GROUP, SSM_HEAD_DIM, SSM_STATE
    xf = x.astype(jnp.float32).reshape(bsz, nc, cs, G, R, P)
    dtc = dt.reshape(bsz, nc, cs, G, R)
    bc = bmat.astype(jnp.float32).reshape(bsz, nc, cs, G, N)
    cc = cmat.astype(jnp.float32).reshape(bsz, nc, cs, G, N)
    acs_l = jnp.cumsum(dtc * a.reshape(G, R), axis=2)
    acs = jnp.moveaxis(acs_l, 2, -1)
    xdt = xf * dtc[..., None]
    tril = jnp.tril(jnp.ones((cs, cs), dtype=bool))
    seg = acs[..., :, None] - acs[..., None, :]
    decay = jnp.exp(jnp.where(tril, seg, -jnp.inf))
    cb = jnp.einsum('bclgn,bcsgn->bcgls', cc, bc)
    att = cb[:, :, :, None] * decay
    y_diag = jnp.einsum('bcgrls,bcsgrp->bclgrp', att, xdt)
    decay_to_end = jnp.exp(acs_l[:, :, -1:] - acs_l)
    states = jnp.einsum('bclgn,bclgrp->bcgrpn', bc, xdt * decay_to_end[..., None])
    chunk_decay = jnp.exp(acs_l[:, :, -1])

    def step(h, inp):
        st, dec = inp
        return h * dec[..., None, None] + st, h

    h_init = h0.astype(jnp.float32).reshape(bsz, G, R, P, N)
    h_last, h_starts = lax.scan(step, h_init, (jnp.moveaxis(states, 1, 0), jnp.moveaxis(chunk_decay, 1, 0)))
    h_starts = jnp.moveaxis(h_starts, 0, 1)
    y_off = jnp.einsum('bclgn,bcgrpn->bclgrp', cc, h_starts) * jnp.exp(acs_l)[..., None]
    y = (y_diag + y_off).reshape(bsz, L, SSM_HEADS, P)
    return y, h_last.reshape(bsz, SSM_HEADS, P, N)


def spatial_gating(u, v, ws, bs):
    bsz, L, _ = v.shape
    nc = -(-L // GM_CHUNK)
    lp_ = nc * GM_CHUNK
    vp = jnp.pad(v, ((0, 0), (0, lp_ - L), (0, 0))).reshape(bsz, nc, GM_CHUNK, GM_GROUPS, GM_GROUP_DIM)
    w_causal = jnp.where(jnp.tril(jnp.ones((GM_CHUNK, GM_CHUNK), dtype=bool)), ws, 0.0).astype(v.dtype)
    s = jnp.einsum('gts,bcsgk->bctgk', w_causal, vp) + bs.T[:, :, None].astype(v.dtype)
    s = s.reshape(bsz, lp_, GM_WIDTH)[:, :L]
    return u * s


def hier_moe(x, w_rg, b_rg, w_re, b_re, w_eg, w_eu, w_ed):
    shp = x.shape
    xt = x.reshape(-1, D_MODEL)
    T = xt.shape[0]
    lg = (xt @ w_rg).astype(jnp.float32) + b_rg.astype(jnp.float32)
    pg = jax.nn.softmax(lg, axis=-1)
    gid = jnp.argmax(lg, axis=-1).astype(jnp.int32)
    gw = jnp.take_along_axis(pg, gid[:, None], axis=1)[:, 0]
    le = (xt @ w_re).astype(jnp.float32).reshape(T, MOE_GROUPS, MOE_EXPERTS_PER_GROUP) + b_re.astype(jnp.float32)
    le_sel = jnp.take_along_axis(le, gid[:, None, None], axis=1)[:, 0]
    top_v, top_i = lax.top_k(le_sel, MOE_TOP_K)
    wts = gw[:, None] * jax.nn.softmax(top_v, axis=-1)
    eid = gid[:, None] * MOE_EXPERTS_PER_GROUP + top_i.astype(jnp.int32)
    flat_e = eid.reshape(-1)
    flat_tok = jnp.repeat(jnp.arange(T, dtype=jnp.int32), MOE_TOP_K)
    flat_w = wts.reshape(-1)
    order = jnp.argsort(flat_e)
    se, stok, sw = flat_e[order], flat_tok[order], flat_w[order]
    counts = jax.ops.segment_sum(jnp.ones_like(flat_e), flat_e, num_segments=MOE_EXPERTS)
    starts = jnp.cumsum(counts) - counts
    padded = (counts + MOE_BLOCK - 1) // MOE_BLOCK * MOE_BLOCK
    pad_ends = jnp.cumsum(padded)
    pad_starts = pad_ends - padded
    dest = pad_starts[se] + jnp.arange(T * MOE_TOP_K, dtype=jnp.int32) - starts[se]
    n_blocks = -(-(T * MOE_TOP_K) // MOE_BLOCK) + MOE_EXPERTS
    n_slots = n_blocks * MOE_BLOCK
    slot_tok = jnp.full((n_slots,), T, dtype=jnp.int32).at[dest].set(stok)
    slot_w = jnp.zeros((n_slots,), jnp.float32).at[dest].set(sw)
    block_e = jnp.minimum(
        jnp.searchsorted(pad_ends, jnp.arange(n_blocks, dtype=jnp.int32) * MOE_BLOCK, side='right'),
        MOE_EXPERTS - 1)
    x_pad = jnp.concatenate([xt, jnp.zeros((1, D_MODEL), xt.dtype)], axis=0)

    def expert_block(args):
        toks, e = args
        xb = x_pad[toks]
        h = jax.nn.silu(xb @ w_eg[e]) * (xb @ w_eu[e])
        return h @ w_ed[e]

    out = lax.map(expert_block, (slot_tok.reshape(n_blocks, MOE_BLOCK), block_e)).reshape(n_slots, D_MODEL)
    y = jnp.zeros((T + 1, D_MODEL), jnp.float32).at[slot_tok].add(out.astype(jnp.float32) * slot_w[:, None])
    return y[:T].astype(x.dtype).reshape(shp)


def memory_kv(mem, g, wk, wv):
    mn = rmsnorm(mem, g)
    b, m = mem.shape[:2]
    k = (mn @ wk).reshape(b, m, MEM_HEADS, MEM_HEAD_DIM)
    v = (mn @ wv).reshape(b, m, MEM_HEADS, MEM_HEAD_DIM)
    return k, v


def cross_attend(xn, k, v, wq, wo):
    b, L = xn.shape[:2]
    q = (xn @ wq).reshape(b, L, MEM_HEADS, MEM_HEAD_DIM)
    s = jnp.einsum('blhd,bmhd->bhlm', q, k.astype(q.dtype)).astype(jnp.float32) * (MEM_HEAD_DIM ** -0.5)
    p = jax.nn.softmax(s, axis=-1).astype(q.dtype)
    o = jnp.einsum('bhlm,bmhd->blhd', p, v.astype(q.dtype)).reshape(b, L, D_MODEL)
    return o @ wo


def mixer_block(xn, conv_prev, h0, lp):
    bsz, L = xn.shape[:2]
    proj = xn @ lp['w_in']
    s1 = SSM_D_INNER
    s2 = s1 + SSM_CONV_DIM
    s3 = s2 + SSM_HEADS
    z, xbc, dt_raw, gm = proj[..., :s1], proj[..., s1:s2], proj[..., s2:s3], proj[..., s3:]
    xbc, conv_new = causal_depthwise_conv(xbc, conv_prev, lp['conv_w'], lp['conv_b'])
    xbc = jax.nn.silu(xbc)
    nb = SSM_GROUPS * SSM_STATE
    xs = xbc[..., :SSM_D_INNER].reshape(bsz, L, SSM_HEADS, SSM_HEAD_DIM)
    bm = xbc[..., SSM_D_INNER:SSM_D_INNER + nb].reshape(bsz, L, SSM_GROUPS, SSM_STATE)
    cm = xbc[..., SSM_D_INNER + nb:].reshape(bsz, L, SSM_GROUPS, SSM_STATE)
    dt = jax.nn.softplus(dt_raw.astype(jnp.float32) + lp['dt_bias'].astype(jnp.float32))
    a = -jnp.exp(lp['a_log'].astype(jnp.float32))
    y, h_new = ssd_scan(xs, dt, a, bm, cm, h0)
    y = y + lp['d_skip'].astype(jnp.float32)[:, None] * xs.astype(jnp.float32)
    y = gated_rmsnorm(y.reshape(bsz, L, SSM_D_INNER), z, lp['ssm_norm'])
    a_out = y @ lp['w_a_proj']
    gm = jax.nn.gelu(gm, approximate=False)
    u, v = gm[..., :GM_WIDTH], gm[..., GM_WIDTH:]
    v = layernorm(v, lp['gm_ln_g'], lp['gm_ln_b'])
    b_out = spatial_gating(u, v, lp['gm_ws'], lp['gm_bs']) @ lp['w_b_proj']
    gates = jax.nn.sigmoid(xn @ lp['w_gate'] + lp['b_gate'])
    merged = gates[..., :D_MODEL] * a_out + gates[..., D_MODEL:] * b_out
    return merged @ lp['w_out'], conv_new, h_new, v


def decoder_layer(x, mem_k, mem_v, conv_prev, h0, lp):
    out, conv_new, h_new, v_rows = mixer_block(rmsnorm(x, lp['norm_mix']), conv_prev, h0, lp)
    x = x + out
    x = x + cross_attend(rmsnorm(x, lp['norm_mem']), mem_k, mem_v, lp['w_mem_q'], lp['w_mem_o'])
    x = x + hier_moe(rmsnorm(x, lp['norm_ffn']), lp['w_router_group'], lp['b_router_group'],
                     lp['w_router_expert'], lp['b_router_expert'],
                     lp['w_expert_gate'], lp['w_expert_up'], lp['w_expert_down'])
    return x, conv_new, h_new, v_rows


def setup_inputs(seed: int = 0) -> dict:
    key = jax.random.key(seed)
    ks = iter(jax.random.split(key, 48))

    def nrm(shape, scale):
        return scale * jax.random.normal(next(ks), shape, jnp.float32)

    dt0 = jnp.exp(jax.random.uniform(next(ks), (DEPTH, SSM_HEADS), jnp.float32,
                                     minval=math.log(1e-3), maxval=math.log(1e-1)))
    dt_bias = dt0 + jnp.log(-jnp.expm1(-dt0))
    a_log = jnp.log(jax.random.uniform(next(ks), (DEPTH, SSM_HEADS), jnp.float32, minval=1.0, maxval=16.0))
    return {
        'x_prompt': nrm((BATCH, SEQ, D_MODEL), 1.0),
        'x_sample': nrm((DEC_BATCH, DEC_SEQ, D_MODEL), 1.0),
        'mem_prompt': nrm((BATCH, MEM_LEN, D_MODEL), 1.0),
        'state_ssm': nrm((DEPTH, DEC_BATCH, SSM_HEADS, SSM_HEAD_DIM, SSM_STATE), 0.3),
        'state_conv': nrm((DEPTH, DEC_BATCH, SSM_CONV - 1, SSM_CONV_DIM), 1.0),
        'cache_mem_k': nrm((DEPTH, DEC_BATCH, MEM_LEN, MEM_HEADS, MEM_HEAD_DIM), 1.0),
        'cache_mem_v': nrm((DEPTH, DEC_BATCH, MEM_LEN, MEM_HEADS, MEM_HEAD_DIM), 1.0),
        'norm_mix': 1.0 + nrm((DEPTH, D_MODEL), 0.05),
        'w_in': nrm((DEPTH, D_MODEL, IN_DIM), D_MODEL ** -0.5),
        'conv_w': nrm((DEPTH, SSM_CONV, SSM_CONV_DIM), 0.5),
        'conv_b': nrm((DEPTH, SSM_CONV_DIM), 0.02),
        'dt_bias': dt_bias,
        'a_log': a_log,
        'd_skip': 1.0 + nrm((DEPTH, SSM_HEADS), 0.1),
        'ssm_norm': 1.0 + nrm((DEPTH, SSM_D_INNER), 0.05),
        'w_a_proj': nrm((DEPTH, SSM_D_INNER, D_MODEL), SSM_D_INNER ** -0.5),
        'gm_ln_g': 1.0 + nrm((DEPTH, GM_WIDTH), 0.05),
        'gm_ln_b': nrm((DEPTH, GM_WIDTH), 0.02),
        'gm_ws': nrm((DEPTH, GM_GROUPS, GM_CHUNK, GM_CHUNK), GM_CHUNK ** -0.5),
        'gm_bs': 1.0 + nrm((DEPTH, GM_GROUPS, GM_CHUNK), 0.1),
        'w_b_proj': nrm((DEPTH, GM_WIDTH, D_MODEL), GM_WIDTH ** -0.5),
        'w_gate': nrm((DEPTH, D_MODEL, 2 * D_MODEL), D_MODEL ** -0.5),
        'b_gate': nrm((DEPTH, 2 * D_MODEL), 0.1),
        'w_out': nrm((DEPTH, D_MODEL, D_MODEL), D_MODEL ** -0.5),
        'norm_mem': 1.0 + nrm((DEPTH, D_MODEL), 0.05),
        'norm_memkv': 1.0 + nrm((DEPTH, D_MODEL), 0.05),
        'w_mem_q': nrm((DEPTH, D_MODEL, D_MODEL), D_MODEL ** -0.5),
        'w_mem_k': nrm((DEPTH, D_MODEL, D_MODEL), D_MODEL ** -0.5),
        'w_mem_v': nrm((DEPTH, D_MODEL, D_MODEL), D_MODEL ** -0.5),
        'w_mem_o': nrm((DEPTH, D_MODEL, D_MODEL), D_MODEL ** -0.5),
        'norm_ffn': 1.0 + nrm((DEPTH, D_MODEL), 0.05),
        'w_router_group': nrm((DEPTH, D_MODEL, MOE_GROUPS), D_MODEL ** -0.5),
        'b_router_group': nrm((DEPTH, MOE_GROUPS), 0.01),
        'w_router_expert': nrm((DEPTH, D_MODEL, MOE_EXPERTS), D_MODEL ** -0.5),
        'b_router_expert': nrm((DEPTH, MOE_GROUPS, MOE_EXPERTS_PER_GROUP), 0.01),
        'w_expert_gate': nrm((DEPTH, MOE_EXPERTS, D_MODEL, MOE_D_FF), D_MODEL ** -0.5),
        'w_expert_up': nrm((DEPTH, MOE_EXPERTS, D_MODEL, MOE_D_FF), D_MODEL ** -0.5),
        'w_expert_down': nrm((DEPTH, MOE_EXPERTS, MOE_D_FF, D_MODEL), MOE_D_FF ** -0.5),
        'norm_final': 1.0 + nrm((D_MODEL,), 0.05),
    }


def reference(x_prompt, x_sample, mem_prompt, state_ssm, state_conv, cache_mem_k, cache_mem_v,
              norm_mix, w_in, conv_w, conv_b, dt_bias, a_log, d_skip, ssm_norm, w_a_proj,
              gm_ln_g, gm_ln_b, gm_ws, gm_bs, w_b_proj, w_gate, b_gate, w_out,
              norm_mem, norm_memkv, w_mem_q, w_mem_k, w_mem_v, w_mem_o,
              norm_ffn, w_router_group, b_router_group, w_router_expert, b_router_expert,
              w_expert_gate, w_expert_up, w_expert_down, norm_final):
    xp, xs = x_prompt, x_sample
    bp = x_prompt.shape[0]
    ssm_p, conv_p, mk_p, mv_p = [], [], [], []
    ssm_s, conv_s, gv_s = [], [], []
    for l in range(DEPTH):
        lp = {
            'norm_mix': norm_mix[l], 'w_in': w_in[l], 'conv_w': conv_w[l], 'conv_b': conv_b[l],
            'dt_bias': dt_bias[l], 'a_log': a_log[l], 'd_skip': d_skip[l], 'ssm_norm': ssm_norm[l],
            'w_a_proj': w_a_proj[l], 'gm_ln_g': gm_ln_g[l], 'gm_ln_b': gm_ln_b[l], 'gm_ws': gm_ws[l],
            'gm_bs': gm_bs[l], 'w_b_proj': w_b_proj[l], 'w_gate': w_gate[l], 'b_gate': b_gate[l],
            'w_out': w_out[l], 'norm_mem': norm_mem[l], 'w_mem_q': w_mem_q[l], 'w_mem_o': w_mem_o[l],
            'norm_ffn': norm_ffn[l], 'w_router_group': w_router_group[l],
            'b_router_group': b_router_group[l], 'w_router_expert': w_router_expert[l],
            'b_router_expert': b_router_expert[l], 'w_expert_gate': w_expert_gate[l],
            'w_expert_up': w_expert_up[l], 'w_expert_down': w_expert_down[l],
        }
        k_p, v_p = memory_kv(mem_prompt, norm_memkv[l], w_mem_k[l], w_mem_v[l])
        conv0 = jnp.zeros((bp, SSM_CONV - 1, SSM_CONV_DIM), xp.dtype)
        h0 = jnp.zeros((bp, SSM_HEADS, SSM_HEAD_DIM, SSM_STATE), jnp.float32)
        xp, c_new, h_new, _ = decoder_layer(xp, k_p, v_p, conv0, h0, lp)
        ssm_p.append(h_new)
        conv_p.append(c_new)
        mk_p.append(k_p)
        mv_p.append(v_p)
        xs, c_new_s, h_new_s, v_rows_s = decoder_layer(xs, cache_mem_k[l], cache_mem_v[l],
                                                       state_conv[l], state_ssm[l], lp)
        ssm_s.append(h_new_s)
        conv_s.append(c_new_s)
        gv_s.append(v_rows_s)
    y_prompt = rmsnorm(xp, norm_final)
    y_sample = rmsnorm(xs, norm_final)
    return (y_prompt, y_sample,
            jnp.stack(ssm_p), jnp.stack(conv_p), jnp.stack(mk_p), jnp.stack(mv_p),
            jnp.stack(ssm_s), jnp.stack(conv_s), jnp.stack(gv_s))
```

```python
import functools

import jax
import jax.numpy as jnp
from jax import lax
from jax.experimental import pallas as pl
from jax.experimental.pallas import tpu as pltpu

F32 = jnp.float32
BF16 = jnp.bfloat16
HIGHEST = lax.Precision.HIGHEST

NORM_EPS = 1e-6
D_MODEL = 1024
SSM_D_INNER = 1536
SSM_HEAD_DIM = 64
SSM_HEADS = 24
SSM_GROUPS = 4
SSM_HEADS_PER_GROUP = 6
SSM_STATE = 128
SSM_CONV = 4
SSM_CONV_DIM = 2560
SSM_GROUP_WIDTH = SSM_D_INNER // SSM_GROUPS
GM_WIDTH = 512
GM_GROUPS = 4
GM_GROUP_DIM = 128
MEM_HEADS = 4
MEM_HEAD_DIM = 256
MOE_GROUPS = 4
MOE_EXPERTS_PER_GROUP = 4
MOE_EXPERTS = 16
MOE_TOP_K = 2
MOE_D_FF = 512

LANES = 128
SUBLANES = 8
SAMPLE_ROWS = 8
PROMPT_CHUNK = 128
MOE_ROWS = 256
VMEM_LIMIT = 48 * 1024 * 1024

NT_DIMS = (((1,), (1,)), ((), ()))
TN_DIMS = (((0,), (0,)), ((), ()))


def _params(n_axes):
    return pltpu.CompilerParams(dimension_semantics=("arbitrary",) * n_axes, vmem_limit_bytes=VMEM_LIMIT)


def _rms(x, g):
    return x * lax.rsqrt(jnp.mean(x * x, axis=-1, keepdims=True) + NORM_EPS) * g


def _silu(x):
    return x * jax.nn.sigmoid(x)


def _gelu(x):
    return 0.5 * x * (1.0 + lax.erf(x * (2.0 ** -0.5)))


def _bdot(a, b):
    return jnp.dot(a.astype(BF16), b.astype(BF16), preferred_element_type=F32)


def _full(shape):
    n = len(shape)
    return pl.BlockSpec(shape, lambda *_: (0,) * n)


def _inproj_body(x_ref, g_ref, wz_ref, wx_ref, wdt_ref, wu_ref, wv_ref, lng_ref, lnb_ref,
                 z_ref, xbc_ref, dt_ref, u_ref, v_ref):
    xn = _rms(x_ref[...], g_ref[...]).astype(BF16)
    z_ref[...] = jnp.dot(xn, wz_ref[...], preferred_element_type=F32)
    xbc_ref[...] = jnp.dot(xn, wx_ref[...], preferred_element_type=F32)
    dt_ref[...] = jnp.dot(xn, wdt_ref[...], preferred_element_type=F32)
    u_ref[...] = _gelu(jnp.dot(xn, wu_ref[...], preferred_element_type=F32))
    v = _gelu(jnp.dot(xn, wv_ref[...], preferred_element_type=F32))
    vc = v - jnp.mean(v, axis=-1, keepdims=True)
    var = jnp.mean(vc * vc, axis=-1, keepdims=True)
    v_ref[...] = vc * lax.rsqrt(var + NORM_EPS) * lng_ref[...] + lnb_ref[...]


def _inproj(x, w, tm):
    t = x.shape[0]
    row = lambda n: pl.BlockSpec((tm, n), lambda i: (i, 0))
    outs = [SSM_D_INNER, SSM_CONV_DIM, LANES, GM_WIDTH, GM_WIDTH]
    return pl.pallas_call(
        _inproj_body,
        grid=(t // tm,),
        in_specs=[row(D_MODEL), _full((1, D_MODEL)), _full((D_MODEL, SSM_D_INNER)), _full((D_MODEL, SSM_CONV_DIM)),
                  _full((D_MODEL, LANES)), _full((D_MODEL, GM_WIDTH)), _full((D_MODEL, GM_WIDTH)),
                  _full((1, GM_WIDTH)), _full((1, GM_WIDTH))],
        out_specs=[row(n) for n in outs],
        out_shape=[jax.ShapeDtypeStruct((t, n), F32) for n in outs],
        compiler_params=_params(1),
        name="inproj",
    )(x, w["norm_mix"], w["w_z"], w["w_xbc"], w["w_dt"], w["w_u"], w["w_v"], w["gm_ln_g"], w["gm_ln_b"])


def _mixer_body(*refs, lc, lv, has_state):
    if has_state:
        (xbc_ref, dt_ref, z_ref, u_ref, v_ref, cprev_ref, h0_ref, cw_ref, cb_ref, dtb_ref, alog_ref, dsk_ref,
         nw_ref, ws_ref, bst_ref, y_ref, sg_ref, hout_ref, cout_ref, xp_scr, h_scr, y_scr) = refs
    else:
        (xbc_ref, dt_ref, z_ref, u_ref, v_ref, cw_ref, cb_ref, dtb_ref, alog_ref, dsk_ref,
         nw_ref, ws_ref, bst_ref, y_ref, sg_ref, hout_ref, cout_ref, xp_scr, h_scr, y_scr) = refs
    c = pl.program_id(1)
    hist = SSM_CONV - 1
    base = SUBLANES

    @pl.when(c == 0)
    def _():
        if has_state:
            xp_scr[base - hist:base, :] = cprev_ref[...]
            h_scr[...] = h0_ref[...]
        else:
            xp_scr[base - hist:base, :] = jnp.zeros((hist, SSM_CONV_DIM), F32)
            h_scr[...] = jnp.zeros(h_scr.shape, F32)

    xp_scr[base:base + lc, :] = xbc_ref[...]
    xc = cb_ref[...]
    for k in range(SSM_CONV):
        xc = xc + cw_ref[k:k + 1, :] * xp_scr[base - hist + k:base - hist + k + lc, :]
    xc = _silu(xc)
    cout_ref[...] = xp_scr[base + lv - hist:base + lv, :]
    xp_scr[base - hist:base, :] = xp_scr[base + lc - hist:base + lc, :]

    row_i = lax.broadcasted_iota(jnp.int32, (lc, lc), 0)
    col_i = lax.broadcasted_iota(jnp.int32, (lc, lc), 1)
    tril = row_i >= col_i

    dt = jax.nn.softplus(dt_ref[...] + dtb_ref[...])
    if lv < lc:
        dt = jnp.where(lax.broadcasted_iota(jnp.int32, (lc, LANES), 0) < lv, dt, 0.0)
    a = -jnp.exp(alog_ref[...])
    acs = jnp.dot(tril.astype(F32), dt * a, precision=HIGHEST, preferred_element_type=F32)
    acs_t = acs.T
    tot = acs[lc - 1:lc, :]
    e_acs = jnp.exp(acs)
    d_end = jnp.exp(tot - acs)
    c_dec = jnp.exp(tot)

    for g in range(SSM_GROUPS):
        b0 = SSM_D_INNER + g * SSM_STATE
        c0 = SSM_D_INNER + SSM_GROUPS * SSM_STATE + g * SSM_STATE
        bm = xc[:, b0:b0 + SSM_STATE].astype(BF16)
        cm = xc[:, c0:c0 + SSM_STATE].astype(BF16)
        cb = lax.dot_general(cm, bm, NT_DIMS, preferred_element_type=F32)
        for r in range(SSM_HEADS_PER_GROUP):
            h = g * SSM_HEADS_PER_GROUP + r
            p0 = h * SSM_HEAD_DIM
            seg = acs[:, h:h + 1] - acs_t[h:h + 1, :]
            att = cb * jnp.where(tril, jnp.exp(seg), 0.0)
            xs_h = xc[:, p0:p0 + SSM_HEAD_DIM]
            xdt = xs_h * dt[:, h:h + 1]
            h_prev = h_scr[h]
            y_off = lax.dot_general(cm, h_prev.astype(BF16), NT_DIMS, preferred_element_type=F32)
            y_scr[:, p0:p0 + SSM_HEAD_DIM] = (_bdot(att, xdt) + y_off * e_acs[:, h:h + 1]
                                              + dsk_ref[:, p0:p0 + SSM_HEAD_DIM] * xs_h)
            st = lax.dot_general((xdt * d_end[:, h:h + 1]).astype(BF16), bm, TN_DIMS, preferred_element_type=F32)
            h_scr[h] = h_prev * c_dec[:, h:h + 1] + st

    @pl.when(c == pl.num_programs(1) - 1)
    def _():
        hout_ref[...] = h_scr[...]

    yz = y_scr[...] * _silu(z_ref[...])
    for g in range(SSM_GROUPS):
        s0 = g * SSM_GROUP_WIDTH
        part = yz[:, s0:s0 + SSM_GROUP_WIDTH]
        ms = jnp.mean(part * part, axis=-1, keepdims=True)
        y_ref[:, s0:s0 + SSM_GROUP_WIDTH] = part * lax.rsqrt(ms + NORM_EPS) * nw_ref[:, s0:s0 + SSM_GROUP_WIDTH]

    u = u_ref[...]
    v = v_ref[...]
    for g in range(GM_GROUPS):
        k0 = g * GM_GROUP_DIM
        wc = jnp.where(tril, ws_ref[g, 0:lc, 0:lc], 0.0)
        s = _bdot(wc, v[:, k0:k0 + GM_GROUP_DIM]) + bst_ref[0:lc, g:g + 1]
        sg_ref[:, k0:k0 + GM_GROUP_DIM] = u[:, k0:k0 + GM_GROUP_DIM] * s


def _mixer(z, xbc, dt, u, v, w, bn, seq, lc, lv, state):
    nc = seq // lc
    has_state = state is not None
    r3 = lambda a: a.reshape(bn, seq, a.shape[-1])
    blk = lambda n: pl.BlockSpec((None, lc, n), lambda b, c: (b, c, 0))
    in_specs = [blk(SSM_CONV_DIM), blk(LANES), blk(SSM_D_INNER), blk(GM_WIDTH), blk(GM_WIDTH)]
    args = [r3(xbc), r3(dt), r3(z), r3(u), r3(v)]
    if has_state:
        s_ssm, s_conv, layer = state
        in_specs += [pl.BlockSpec((None, None, SSM_CONV - 1, SSM_CONV_DIM), lambda b, c: (layer, b, 0, 0)),
                     pl.BlockSpec((None, None, SSM_HEADS, SSM_HEAD_DIM, SSM_STATE), lambda b, c: (layer, b, 0, 0, 0))]
        args += [s_conv, s_ssm]
    in_specs += [_full((SSM_CONV, SSM_CONV_DIM)), _full((1, SSM_CONV_DIM)), _full((1, LANES)), _full((1, LANES)),
                 _full((1, SSM_D_INNER)), _full((1, SSM_D_INNER)), _full((GM_GROUPS, 128, 128)), _full((128, GM_GROUPS))]
    args += [w["conv_w"], w["conv_b"], w["dt_bias"], w["a_log"], w["d_skip"], w["ssm_norm"], w["gm_ws"], w["gm_bs_t"]]
    y, sg, h_new, conv_new = pl.pallas_call(
        functools.partial(_mixer_body, lc=lc, lv=lv, has_state=has_state),
        grid=(bn, nc),
        in_specs=in_specs,
        out_specs=[blk(SSM_D_INNER), blk(GM_WIDTH),
                   pl.BlockSpec((None, SSM_HEADS, SSM_HEAD_DIM, SSM_STATE), lambda b, c: (b, 0, 0, 0)),
                   pl.BlockSpec((None, SSM_CONV - 1, SSM_CONV_DIM), lambda b, c: (b, 0, 0))],
        out_shape=[jax.ShapeDtypeStruct((bn, seq, SSM_D_INNER), F32), jax.ShapeDtypeStruct((bn, seq, GM_WIDTH), F32),
                   jax.ShapeDtypeStruct((bn, SSM_HEADS, SSM_HEAD_DIM, SSM_STATE), F32),
                   jax.ShapeDtypeStruct((bn, SSM_CONV - 1, SSM_CONV_DIM), F32)],
        scratch_shapes=[pltpu.VMEM((SUBLANES + lc, SSM_CONV_DIM), F32),
                        pltpu.VMEM((SSM_HEADS, SSM_HEAD_DIM, SSM_STATE), F32),
                        pltpu.VMEM((lc, SSM_D_INNER), F32)],
        compiler_params=_params(2),
        name="mixer",
    )(*args)
    t = bn * seq
    return y.reshape(t, SSM_D_INNER), sg.reshape(t, GM_WIDTH), h_new, conv_new


def _merge_body(x_ref, y_ref, sg_ref, gmix_ref, wg_ref, bg_ref, wa_ref, wb_ref, wo_ref, gmem_ref, wq_ref,
                x1_ref, q_ref):
    x = x_ref[...]
    xn = _rms(x, gmix_ref[...]).astype(BF16)
    gates = jax.nn.sigmoid(jnp.dot(xn, wg_ref[...], preferred_element_type=F32) + bg_ref[...])
    a_out = _bdot(y_ref[...], wa_ref[...])
    b_out = _bdot(sg_ref[...], wb_ref[...])
    merged = gates[:, :D_MODEL] * a_out + gates[:, D_MODEL:] * b_out
    x1 = x + _bdot(merged, wo_ref[...])
    x1_ref[...] = x1
    q_ref[...] = _bdot(_rms(x1, gmem_ref[...]), wq_ref[...]).astype(BF16)


def _merge(x, y, sg, w, tm):
    t = x.shape[0]
    row = lambda n: pl.BlockSpec((tm, n), lambda i: (i, 0))
    return pl.pallas_call(
        _merge_body,
        grid=(t // tm,),
        in_specs=[row(D_MODEL), row(SSM_D_INNER), row(GM_WIDTH), _full((1, D_MODEL)),
                  _full((D_MODEL, 2 * D_MODEL)), _full((1, 2 * D_MODEL)), _full((SSM_D_INNER, D_MODEL)),
                  _full((GM_WIDTH, D_MODEL)), _full((D_MODEL, D_MODEL)), _full((1, D_MODEL)), _full((D_MODEL, D_MODEL))],
        out_specs=[row(D_MODEL), row(D_MODEL)],
        out_shape=[jax.ShapeDtypeStruct((t, D_MODEL), F32), jax.ShapeDtypeStruct((t, D_MODEL), BF16)],
        compiler_params=_params(1),
        name="merge",
    )(x, y, sg, w["norm_mix"], w["w_gate"], w["b_gate"], w["w_a_proj"], w["w_b_proj"], w["w_out"],
      w["norm_mem"], w["w_mem_q"])


def _memkv_body(m_ref, g_ref, wk_ref, wv_ref, k_ref, v_ref):
    mn = _rms(m_ref[...], g_ref[...]).astype(BF16)
    k = jnp.dot(mn, wk_ref[...], preferred_element_type=F32)
    v = jnp.dot(mn, wv_ref[...], preferred_element_type=F32)
    for h in range(MEM_HEADS):
        k_ref[:, h, :] = k[:, h * MEM_HEAD_DIM:(h + 1) * MEM_HEAD_DIM]
        v_ref[:, h, :] = v[:, h * MEM_HEAD_DIM:(h + 1) * MEM_HEAD_DIM]


def _memkv(mem, w):
    bn, m, _ = mem.shape
    kv_spec = pl.BlockSpec((None, m, MEM_HEADS, MEM_HEAD_DIM), lambda b: (b, 0, 0, 0))
    kv_shape = jax.ShapeDtypeStruct((bn, m, MEM_HEADS, MEM_HEAD_DIM), F32)
    return pl.pallas_call(
        _memkv_body,
        grid=(bn,),
        in_specs=[pl.BlockSpec((None, m, D_MODEL), lambda b: (b, 0, 0)), _full((1, D_MODEL)),
                  _full((D_MODEL, D_MODEL)), _full((D_MODEL, D_MODEL))],
        out_specs=[kv_spec, kv_spec],
        out_shape=[kv_shape, kv_shape],
        compiler_params=_params(1),
        name="memkv",
    )(mem, w["norm_memkv"], w["w_mem_k"], w["w_mem_v"])


def _attn_body(q_ref, k_ref, v_ref, x_ref, wo_ref, out_ref, o_scr):
    scale = MEM_HEAD_DIM ** -0.5
    for h in range(MEM_HEADS):
        lo = h * MEM_HEAD_DIM
        kh = k_ref[:, h, :].astype(BF16)
        vh = v_ref[:, h, :].astype(BF16)
        s = lax.dot_general(q_ref[:, lo:lo + MEM_HEAD_DIM], kh, NT_DIMS, preferred_element_type=F32) * scale
        e = jnp.exp(s - jnp.max(s, axis=-1, keepdims=True))
        p = e / jnp.sum(e, axis=-1, keepdims=True)
        o_scr[:, lo:lo + MEM_HEAD_DIM] = jnp.dot(p.astype(BF16), vh, preferred_element_type=F32)
    out_ref[...] = x_ref[...] + _bdot(o_scr[...], wo_ref[...])


def _attn(q, k5, v5, layer, x1, w, bn, seq, lq):
    m = k5.shape[2]
    r3 = lambda a: a.reshape(bn, seq, D_MODEL)
    blk = pl.BlockSpec((None, lq, D_MODEL), lambda b, i: (b, i, 0))
    kv_spec = pl.BlockSpec((None, None, m, MEM_HEADS, MEM_HEAD_DIM), lambda b, i: (layer, b, 0, 0, 0))
    out = pl.pallas_call(
        _attn_body,
        grid=(bn, seq // lq),
        in_specs=[blk, kv_spec, kv_spec, blk, _full((D_MODEL, D_MODEL))],
        out_specs=blk,
        out_shape=jax.ShapeDtypeStruct((bn, seq, D_MODEL), F32),
        scratch_shapes=[pltpu.VMEM((lq, D_MODEL), F32)],
        compiler_params=_params(2),
        name="attn",
    )(r3(q), k5, v5, r3(x1), w["w_mem_o"])
    return out.reshape(bn * seq, D_MODEL)


def _first_max(vals):
    m = functools.reduce(jnp.maximum, vals)
    idx = jnp.full(m.shape, len(vals) - 1, jnp.int32)
    for j in range(len(vals) - 2, -1, -1):
        idx = jnp.where(vals[j] == m, j, idx)
    return m, idx


def _route_body(x_ref, g_ref, wr_ref, br_ref, xn_ref, eid_ref, rank_ref, wt_ref, cnt_ref, carry_scr, *, tm):
    @pl.when(pl.program_id(0) == 0)
    def _():
        carry_scr[...] = jnp.zeros(carry_scr.shape, F32)

    xn = _rms(x_ref[...], g_ref[...])
    xn_ref[...] = xn
    lg = lax.dot_general(wr_ref[...], xn, NT_DIMS, precision=HIGHEST, preferred_element_type=F32) + br_ref[...]
    grp = [lg[j:j + 1, :] for j in range(MOE_GROUPS)]
    gmax, gid = _first_max(grp)
    gw = 1.0 / functools.reduce(jnp.add, [jnp.exp(r - gmax) for r in grp])
    ex = [lg[SUBLANES + j:SUBLANES + j + 1, :] for j in range(MOE_EXPERTS)]
    sel = []
    for j in range(MOE_EXPERTS_PER_GROUP):
        pick = ex[(MOE_GROUPS - 1) * MOE_EXPERTS_PER_GROUP + j]
        for g in range(MOE_GROUPS - 2, -1, -1):
            pick = jnp.where(gid == g, ex[g * MOE_EXPERTS_PER_GROUP + j], pick)
        sel.append(pick)
    v1, i1 = _first_max(sel)
    rest = [jnp.where(i1 == j, -jnp.inf, sel[j]) for j in range(MOE_EXPERTS_PER_GROUP)]
    v2, i2 = _first_max(rest)
    e2 = jnp.exp(v2 - v1)
    den = 1.0 + e2
    eid1 = gid * MOE_EXPERTS_PER_GROUP + i1
    eid2 = gid * MOE_EXPERTS_PER_GROUP + i2

    e_iota = lax.broadcasted_iota(jnp.int32, (MOE_EXPERTS, tm), 0)
    m1 = e_iota == eid1
    m2 = e_iota == eid2
    onehot = jnp.where(m1, 1.0, 0.0) + jnp.where(m2, 1.0, 0.0)
    before = lax.broadcasted_iota(jnp.int32, (tm, tm), 0) < lax.broadcasted_iota(jnp.int32, (tm, tm), 1)
    ranks = _bdot(onehot, jnp.where(before, 1.0, 0.0)) + carry_scr[:, 0:1]
    r1 = jnp.sum(jnp.where(m1, ranks, 0.0), axis=0, keepdims=True)
    r2 = jnp.sum(jnp.where(m2, ranks, 0.0), axis=0, keepdims=True)
    carry_scr[...] = carry_scr[...] + jnp.sum(onehot, axis=1, keepdims=True)
    cnt_ref[...] = carry_scr[...]

    zeros_i = jnp.zeros((SUBLANES - MOE_TOP_K, tm), jnp.int32)
    eid_ref[0:1, :] = eid1
    eid_ref[1:2, :] = eid2
    eid_ref[MOE_TOP_K:, :] = zeros_i
    rank_ref[0:1, :] = r1.astype(jnp.int32)
    rank_ref[1:2, :] = r2.astype(jnp.int32)
    rank_ref[MOE_TOP_K:, :] = zeros_i
    wt_ref[0:1, :] = gw / den
    wt_ref[1:2, :] = gw * e2 / den
    wt_ref[MOE_TOP_K:, :] = jnp.zeros((SUBLANES - MOE_TOP_K, tm), F32)


def _route(x, w, tm):
    t = x.shape[0]
    lane_blk = pl.BlockSpec((SUBLANES, tm), lambda i: (0, i))
    nr = w["w_router_t"].shape[0]
    return pl.pallas_call(
        functools.partial(_route_body, tm=tm),
        grid=(t // tm,),
        in_specs=[pl.BlockSpec((tm, D_MODEL), lambda i: (i, 0)), _full((1, D_MODEL)), _full((nr, D_MODEL)), _full((nr, 1))],
        out_specs=[pl.BlockSpec((tm, D_MODEL), lambda i: (i, 0)), lane_blk, lane_blk, lane_blk,
                   _full((MOE_EXPERTS, LANES))],
        out_shape=[jax.ShapeDtypeStruct((t, D_MODEL), F32), jax.ShapeDtypeStruct((SUBLANES, t), jnp.int32),
                   jax.ShapeDtypeStruct((SUBLANES, t), jnp.int32), jax.ShapeDtypeStruct((SUBLANES, t), F32),
                   jax.ShapeDtypeStruct((MOE_EXPERTS, LANES), F32)],
        scratch_shapes=[pltpu.VMEM((MOE_EXPERTS, LANES), F32)],
        compiler_params=_params(1),
        name="route",
    )(x, w["norm_ffn"], w["w_router_t"], w["b_router_t"])


def _row_copy(src, src_row, dst, dst_row, sem):
    return pltpu.make_async_copy(src.at[pl.ds(src_row, 1), :], dst.at[pl.ds(dst_row, 1), :], sem)


def _dispatch_body(dest_ref, xn_ref, init_ref, out_ref, sem, *, tm):
    del init_ref

    def issue(t, carry):
        for k in range(MOE_TOP_K):
            _row_copy(xn_ref, t, out_ref, dest_ref[k, t], sem).start()
        return carry

    def drain(t, carry):
        for k in range(MOE_TOP_K):
            _row_copy(xn_ref, t, out_ref, dest_ref[k, t], sem).wait()
        return carry

    lax.fori_loop(0, tm, issue, 0)
    lax.fori_loop(0, tm, drain, 0)


def _dispatch(xn, dest, n_slots, tm):
    t = xn.shape[0]
    return pl.pallas_call(
        functools.partial(_dispatch_body, tm=tm),
        grid=(t // tm,),
        in_specs=[pl.BlockSpec((MOE_TOP_K, tm), lambda i: (0, i), memory_space=pltpu.SMEM),
                  pl.BlockSpec((tm, D_MODEL), lambda i: (i, 0)),
                  pl.BlockSpec(memory_space=pl.ANY)],
        out_specs=pl.BlockSpec(memory_space=pl.ANY),
        out_shape=jax.ShapeDtypeStruct((n_slots, D_MODEL), F32),
        scratch_shapes=[pltpu.SemaphoreType.DMA(())],
        input_output_aliases={2: 0},
        compiler_params=_params(1),
        name="dispatch",
    )(dest, xn, jnp.zeros((n_slots, D_MODEL), F32))


def _ffn_body(be_ref, nu_ref, x_ref, wgu_ref, wd_ref, o_ref):
    del be_ref

    used = pl.program_id(0) < nu_ref[0]

    @pl.when(used)
    def _():
        gu = _bdot(x_ref[...], wgu_ref[...])
        hid = _silu(gu[:, :MOE_D_FF]) * gu[:, MOE_D_FF:]
        o_ref[...] = _bdot(hid, wd_ref[...])

    @pl.when(jnp.logical_not(used))
    def _():
        o_ref[...] = jnp.zeros(o_ref.shape, F32)


def _ffn(xs, block_e, n_used, w):
    n_slots = xs.shape[0]
    nb = n_slots // MOE_ROWS
    blk_i = lambda i, be, nu: (jnp.minimum(i, nu[0] - 1), 0)
    exp_i = lambda i, be, nu: (be[jnp.minimum(i, nu[0] - 1)], 0, 0)
    grid_spec = pltpu.PrefetchScalarGridSpec(
        num_scalar_prefetch=2,
        grid=(nb,),
        in_specs=[pl.BlockSpec((MOE_ROWS, D_MODEL), blk_i),
                  pl.BlockSpec((None, D_MODEL, 2 * MOE_D_FF), exp_i),
                  pl.BlockSpec((None, MOE_D_FF, D_MODEL), exp_i)],
        out_specs=pl.BlockSpec((MOE_ROWS, D_MODEL), lambda i, be, nu: (i, 0)),
    )
    return pl.pallas_call(
        _ffn_body,
        grid_spec=grid_spec,
        out_shape=jax.ShapeDtypeStruct((n_slots, D_MODEL), F32),
        compiler_params=_params(1),
        name="ffn",
    )(block_e, n_used, xs, w["w_expert_gu"], w["w_expert_down"])


def _combine_body(dest_ref, wt_ref, x_ref, ys_ref, gfin_ref, out_ref, buf0, buf1, sem, *, tm, final):
    bufs = (buf0, buf1)

    def issue(t, carry):
        for k in range(MOE_TOP_K):
            _row_copy(ys_ref, dest_ref[k, t], bufs[k], t, sem).start()
        return carry

    def drain(t, carry):
        for k in range(MOE_TOP_K):
            _row_copy(ys_ref, dest_ref[k, t], bufs[k], t, sem).wait()
        return carry

    lax.fori_loop(0, tm, issue, 0)
    lax.fori_loop(0, tm, drain, 0)
    wt = wt_ref[...].T
    out = x_ref[...] + (wt[:, 0:1] * buf0[...] + wt[:, 1:2] * buf1[...])
    if final:
        out = _rms(out, gfin_ref[...])
    out_ref[...] = out


def _combine(x, ys, dest, wt, g_final, tm, final):
    t = x.shape[0]
    return pl.pallas_call(
        functools.partial(_combine_body, tm=tm, final=final),
        grid=(t // tm,),
        in_specs=[pl.BlockSpec((MOE_TOP_K, tm), lambda i: (0, i), memory_space=pltpu.SMEM),
                  pl.BlockSpec((SUBLANES, tm), lambda i: (0, i)),
                  pl.BlockSpec((tm, D_MODEL), lambda i: (i, 0)),
                  pl.BlockSpec(memory_space=pl.ANY),
                  _full((1, D_MODEL))],
        out_specs=pl.BlockSpec((tm, D_MODEL), lambda i: (i, 0)),
        out_shape=jax.ShapeDtypeStruct((t, D_MODEL), F32),
        scratch_shapes=[pltpu.VMEM((tm, D_MODEL), F32), pltpu.VMEM((tm, D_MODEL), F32), pltpu.SemaphoreType.DMA(())],
        compiler_params=_params(1),
        name="combine",
    )(dest, wt, x, ys, g_final)


def _moe(x, w, g_final, final, tm):
    t = x.shape[0]
    xn, eid, rank, wt, cnt = _route(x, w, tm)
    counts = cnt[:, 0].astype(jnp.int32)
    padded = (counts + MOE_ROWS - 1) // MOE_ROWS * MOE_ROWS
    pad_ends = jnp.cumsum(padded)
    pad_starts = pad_ends - padded
    n_blocks = (t * MOE_TOP_K) // MOE_ROWS + MOE_EXPERTS
    dest = pad_starts[eid[:MOE_TOP_K]] + rank[:MOE_TOP_K]
    block_e = jnp.minimum(
        jnp.searchsorted(pad_ends, jnp.arange(n_blocks, dtype=jnp.int32) * MOE_ROWS, side="right"),
        MOE_EXPERTS - 1).astype(jnp.int32)
    n_used = (pad_ends[-1:] // MOE_ROWS).astype(jnp.int32)
    xs = _dispatch(xn, dest, n_blocks * MOE_ROWS, tm)
    ys = _ffn(xs, block_e, n_used, w)
    return _combine(x, ys, dest, wt, g_final, tm, final)


def _layer(x, k5, v5, kv_layer, state, w, g_final, final, bn, seq, lc, lv, lq, tm, tm_moe):
    z, xbc, dt, u, v = _inproj(x, w, tm)
    y, sg, h_new, conv_new = _mixer(z, xbc, dt, u, v, w, bn, seq, lc, lv, state)
    x1, q = _merge(x, y, sg, w, tm)
    x2 = _attn(q, k5, v5, kv_layer, x1, w, bn, seq, lq)
    x3 = _moe(x2, w, g_final, final, tm_moe)
    return x3, h_new, conv_new, v


def _layer_weights(l, p):
    row = lambda a: a[l].reshape(1, -1)
    s1 = SSM_D_INNER
    s2 = s1 + SSM_CONV_DIM
    s3 = s2 + SSM_HEADS
    w_in = p["w_in"][l]
    pad_h = LANES - SSM_HEADS
    wr = jnp.zeros((SUBLANES + MOE_EXPERTS, D_MODEL), F32)
    wr = wr.at[:MOE_GROUPS].set(p["w_router_group"][l].T).at[SUBLANES:].set(p["w_router_expert"][l].T)
    br = jnp.zeros((SUBLANES + MOE_EXPERTS, 1), F32)
    br = br.at[:MOE_GROUPS, 0].set(p["b_router_group"][l]).at[SUBLANES:, 0].set(p["b_router_expert"][l].reshape(-1))
    return {
        "norm_mix": row(p["norm_mix"]),
        "w_z": w_in[:, :s1].astype(BF16),
        "w_xbc": w_in[:, s1:s2].astype(BF16),
        "w_dt": jnp.pad(w_in[:, s2:s3], ((0, 0), (0, pad_h))).astype(BF16),
        "w_u": w_in[:, s3:s3 + GM_WIDTH].astype(BF16),
        "w_v": w_in[:, s3 + GM_WIDTH:].astype(BF16),
        "gm_ln_g": row(p["gm_ln_g"]), "gm_ln_b": row(p["gm_ln_b"]),
        "conv_w": p["conv_w"][l], "conv_b": row(p["conv_b"]),
        "dt_bias": jnp.pad(row(p["dt_bias"]), ((0, 0), (0, pad_h))),
        "a_log": jnp.pad(row(p["a_log"]), ((0, 0), (0, pad_h))),
        "d_skip": jnp.repeat(p["d_skip"][l], SSM_HEAD_DIM).reshape(1, -1),
        "ssm_norm": row(p["ssm_norm"]),
        "gm_ws": p["gm_ws"][l], "gm_bs_t": p["gm_bs"][l].T,
        "w_gate": p["w_gate"][l].astype(BF16), "b_gate": row(p["b_gate"]),
        "w_a_proj": p["w_a_proj"][l].astype(BF16), "w_b_proj": p["w_b_proj"][l].astype(BF16),
        "w_out": p["w_out"][l].astype(BF16),
        "norm_mem": row(p["norm_mem"]), "norm_memkv": row(p["norm_memkv"]),
        "w_mem_q": p["w_mem_q"][l].astype(BF16), "w_mem_k": p["w_mem_k"][l].astype(BF16),
        "w_mem_v": p["w_mem_v"][l].astype(BF16), "w_mem_o": p["w_mem_o"][l].astype(BF16),
        "norm_ffn": row(p["norm_ffn"]),
        "w_router_t": wr, "b_router_t": br,
        "w_expert_gu": jnp.concatenate([p["w_expert_gate"][l], p["w_expert_up"][l]], axis=-1).astype(BF16),
        "w_expert_down": p["w_expert_down"][l].astype(BF16),
    }


def kernel(x_prompt, x_sample, mem_prompt, state_ssm, state_conv, cache_mem_k, cache_mem_v, norm_mix, w_in, conv_w, conv_b, dt_bias, a_log, d_skip, ssm_norm, w_a_proj, gm_ln_g, gm_ln_b, gm_ws, gm_bs, w_b_proj, w_gate, b_gate, w_out, norm_mem, norm_memkv, w_mem_q, w_mem_k, w_mem_v, w_mem_o, norm_ffn, w_router_group, b_router_group, w_router_expert, b_router_expert, w_expert_gate, w_expert_up, w_expert_down, norm_final):
    p = dict(norm_mix=norm_mix, w_in=w_in, conv_w=conv_w, conv_b=conv_b, dt_bias=dt_bias, a_log=a_log, d_skip=d_skip,
             ssm_norm=ssm_norm, w_a_proj=w_a_proj, gm_ln_g=gm_ln_g, gm_ln_b=gm_ln_b, gm_ws=gm_ws, gm_bs=gm_bs,
             w_b_proj=w_b_proj, w_gate=w_gate, b_gate=b_gate, w_out=w_out, norm_mem=norm_mem, norm_memkv=norm_memkv,
             w_mem_q=w_mem_q, w_mem_k=w_mem_k, w_mem_v=w_mem_v, w_mem_o=w_mem_o, norm_ffn=norm_ffn,
             w_router_group=w_router_group, b_router_group=b_router_group, w_router_expert=w_router_expert,
             b_router_expert=b_router_expert, w_expert_gate=w_expert_gate, w_expert_up=w_expert_up,
             w_expert_down=w_expert_down)
    depth = w_in.shape[0]
    bp, lp, _ = x_prompt.shape
    bs, ls, _ = x_sample.shape
    assert lp % PROMPT_CHUNK == 0 and ls <= SAMPLE_ROWS
    g_final = norm_final.reshape(1, -1)

    xp = x_prompt.reshape(bp * lp, D_MODEL)
    xs = jnp.pad(x_sample, ((0, 0), (0, SAMPLE_ROWS - ls), (0, 0))).reshape(bs * SAMPLE_ROWS, D_MODEL)
    tp, ts = bp * lp, bs * SAMPLE_ROWS
    tm_p, tm_s = min(256, tp), min(256, ts)
    tmm_p, tmm_s = min(512, tp), min(512, ts)
    lq_p = min(512, lp)

    ssm_p, conv_p, mk_p, mv_p, ssm_s, conv_s, gv_s = [], [], [], [], [], [], []
    for l in range(depth):
        w = _layer_weights(l, p)
        final = l == depth - 1
        k_p, v_p = _memkv(mem_prompt, w)
        xp, h_new, c_new, _ = _layer(xp, k_p[None], v_p[None], 0, None, w, g_final, final,
                                     bp, lp, PROMPT_CHUNK, PROMPT_CHUNK, lq_p, tm_p, tmm_p)
        ssm_p.append(h_new)
        conv_p.append(c_new)
        mk_p.append(k_p)
        mv_p.append(v_p)
        xs, h_new_s, c_new_s, v_s = _layer(xs, cache_mem_k, cache_mem_v, l, (state_ssm, state_conv, l), w, g_final,
                                           final, bs, SAMPLE_ROWS, SAMPLE_ROWS, ls, SAMPLE_ROWS, tm_s, tmm_s)
        ssm_s.append(h_new_s)
        conv_s.append(c_new_s)
        gv_s.append(v_s.reshape(bs, SAMPLE_ROWS, GM_WIDTH)[:, :ls])
    y_prompt = xp.reshape(bp, lp, D_MODEL)
    y_sample = xs.reshape(bs, SAMPLE_ROWS, D_MODEL)[:, :ls]
    return (y_prompt, y_sample, jnp.stack(ssm_p), jnp.stack(conv_p), jnp.stack(mk_p), jnp.stack(mv_p),
            jnp.stack(ssm_s), jnp.stack(conv_s), jnp.stack(gv_s))
```

```python
import functools

import jax
import jax.numpy as jnp
from jax import lax
from jax.experimental import pallas as pl
from jax.experimental.pallas import tpu as pltpu

F32 = jnp.float32
BF16 = jnp.bfloat16
HIGHEST = lax.Precision.HIGHEST

NORM_EPS = 1e-6
D_MODEL = 1024
SSM_D_INNER = 1536
SSM_HEAD_DIM = 64
SSM_HEADS = 24
SSM_GROUPS = 4
SSM_HEADS_PER_GROUP = 6
SSM_STATE = 128
SSM_CONV = 4
SSM_CONV_DIM = 2560
SSM_GROUP_WIDTH = SSM_D_INNER // SSM_GROUPS
GM_WIDTH = 512
GM_GROUPS = 4
GM_GROUP_DIM = 128
MEM_HEADS = 4
MEM_HEAD_DIM = 256
MOE_GROUPS = 4
MOE_EXPERTS_PER_GROUP = 4
MOE_EXPERTS = 16
MOE_TOP_K = 2
MOE_D_FF = 512

LANES = 128
SUBLANES = 8
SAMPLE_ROWS = 8
PROMPT_CHUNK = 128
MOE_ROWS = 256
SPLIT_PARTS = 3
DMA_UNROLL = 8
VMEM_LIMIT = 48 * 1024 * 1024

NT_DIMS = (((1,), (1,)), ((), ()))
TN_DIMS = (((0,), (0,)), ((), ()))


def _params(n_axes, **kw):
    return pltpu.CompilerParams(dimension_semantics=("arbitrary",) * n_axes, vmem_limit_bytes=VMEM_LIMIT, **kw)


def _rms(x, g):
    return x * lax.rsqrt(jnp.mean(x * x, axis=-1, keepdims=True) + NORM_EPS) * g


def _silu(x):
    return x * jax.nn.sigmoid(x)


def _gelu(x):
    return 0.5 * x * (1.0 + lax.erf(x * (2.0 ** -0.5)))


def _bdot(a, b):
    return jnp.dot(a.astype(BF16), b.astype(BF16), preferred_element_type=F32)


def _full(shape):
    n = len(shape)
    return pl.BlockSpec(shape, lambda *_: (0,) * n)


def _inproj_body(x_ref, g_ref, wz_ref, wx_ref, wdt_ref, wu_ref, wv_ref, lng_ref, lnb_ref,
                 z_ref, xbc_ref, dt_ref, u_ref, v_ref):
    xn = _rms(x_ref[...], g_ref[...]).astype(BF16)
    z_ref[...] = jnp.dot(xn, wz_ref[...], preferred_element_type=F32)
    xbc_ref[...] = jnp.dot(xn, wx_ref[...], preferred_element_type=F32)
    dt_ref[...] = jnp.dot(xn, wdt_ref[...], preferred_element_type=F32)
    u_ref[...] = _gelu(jnp.dot(xn, wu_ref[...], preferred_element_type=F32))
    v = _gelu(jnp.dot(xn, wv_ref[...], preferred_element_type=F32))
    vc = v - jnp.mean(v, axis=-1, keepdims=True)
    var = jnp.mean(vc * vc, axis=-1, keepdims=True)
    v_ref[...] = vc * lax.rsqrt(var + NORM_EPS) * lng_ref[...] + lnb_ref[...]


def _inproj(x, w, tm):
    t = x.shape[0]
    row = lambda n: pl.BlockSpec((tm, n), lambda i: (i, 0))
    outs = [SSM_D_INNER, SSM_CONV_DIM, LANES, GM_WIDTH, GM_WIDTH]
    return pl.pallas_call(
        _inproj_body,
        grid=(t // tm,),
        in_specs=[row(D_MODEL), _full((1, D_MODEL)), _full((D_MODEL, SSM_D_INNER)), _full((D_MODEL, SSM_CONV_DIM)),
                  _full((D_MODEL, LANES)), _full((D_MODEL, GM_WIDTH)), _full((D_MODEL, GM_WIDTH)),
                  _full((1, GM_WIDTH)), _full((1, GM_WIDTH))],
        out_specs=[row(n) for n in outs],
        out_shape=[jax.ShapeDtypeStruct((t, n), F32) for n in outs],
        compiler_params=_params(1),
        name="inproj",
    )(x, w["norm_mix"], w["w_z"], w["w_xbc"], w["w_dt"], w["w_u"], w["w_v"], w["gm_ln_g"], w["gm_ln_b"])


def _split_bf16(x):
    parts = []
    rest = x
    for _ in range(SPLIT_PARTS):
        piece = rest.astype(BF16)
        parts.append(piece)
        rest = rest - piece.astype(F32)
    return jnp.concatenate(parts, axis=1)


def _ssd_by_head(xc, dt, acs, acs_t, tril, dsk_ref, h_scr, y_scr):
    lc = xc.shape[0]
    tot = acs[lc - 1:lc, :]
    e_acs = jnp.exp(acs)
    d_end = jnp.exp(tot - acs)
    c_dec = jnp.exp(tot)
    for g in range(SSM_GROUPS):
        b0 = SSM_D_INNER + g * SSM_STATE
        c0 = SSM_D_INNER + SSM_GROUPS * SSM_STATE + g * SSM_STATE
        bm = xc[:, b0:b0 + SSM_STATE].astype(BF16)
        cm = xc[:, c0:c0 + SSM_STATE].astype(BF16)
        cb = lax.dot_general(cm, bm, NT_DIMS, preferred_element_type=F32)
        for r in range(SSM_HEADS_PER_GROUP):
            h = g * SSM_HEADS_PER_GROUP + r
            p0 = h * SSM_HEAD_DIM
            seg = acs[:, h:h + 1] - acs_t[h:h + 1, :]
            att = cb * jnp.where(tril, jnp.exp(seg), 0.0)
            xs_h = xc[:, p0:p0 + SSM_HEAD_DIM]
            xdt = xs_h * dt[:, h:h + 1]
            h_prev = h_scr[p0:p0 + SSM_HEAD_DIM, :]
            y_off = lax.dot_general(cm, h_prev.astype(BF16), NT_DIMS, preferred_element_type=F32)
            y_scr[:, p0:p0 + SSM_HEAD_DIM] = (_bdot(att, xdt) + y_off * e_acs[:, h:h + 1]
                                              + dsk_ref[:, p0:p0 + SSM_HEAD_DIM] * xs_h)
            st = lax.dot_general((xdt * d_end[:, h:h + 1]).astype(BF16), bm, TN_DIMS, preferred_element_type=F32)
            h_scr[p0:p0 + SSM_HEAD_DIM, :] = h_prev * c_dec[:, h:h + 1] + st


def _ssd_wide(xc, dt, acs, tril, dsk_ref, ex_ref, ecol_ref, h_scr, y_scr, acst_scr):
    lc = xc.shape[0]
    acs_parts = _split_bf16(acs)
    wide = jnp.dot(jnp.concatenate([_split_bf16(dt), acs_parts], axis=0), ex_ref[...], preferred_element_type=F32)
    dtx = wide[:lc]
    acsx = wide[lc:]
    colb = jnp.dot(acs_parts, ecol_ref[...], preferred_element_type=F32)
    acst_scr[...] = acs.T
    c_col = jnp.exp(acst_scr[:, lc - 1:lc])
    xs = xc[:, :SSM_D_INNER]
    xdt = xs * dtx
    xdt_b = xdt.astype(BF16)
    xdte_b = (xdt * jnp.exp(acsx[lc - 1:lc, :] - acsx)).astype(BF16)
    e_acsx = jnp.exp(acsx)
    skip = dsk_ref[...] * xs
    first_head = lax.broadcasted_iota(jnp.int32, (lc, LANES), 1) < SSM_HEAD_DIM
    pair = 2 * SSM_HEAD_DIM
    for g in range(SSM_GROUPS):
        b0 = SSM_D_INNER + g * SSM_STATE
        c0 = SSM_D_INNER + SSM_GROUPS * SSM_STATE + g * SSM_STATE
        g0 = g * SSM_GROUP_WIDTH
        bm = xc[:, b0:b0 + SSM_STATE].astype(BF16)
        cm = xc[:, c0:c0 + SSM_STATE].astype(BF16)
        cb = lax.dot_general(cm, bm, NT_DIMS, preferred_element_type=F32)
        h_grp = h_scr[g0:g0 + SSM_GROUP_WIDTH, :]
        y_off = lax.dot_general(cm, h_grp.astype(BF16), NT_DIMS, preferred_element_type=F32)
        for j in range(SSM_HEADS_PER_GROUP // 2):
            h1 = g * SSM_HEADS_PER_GROUP + 2 * j
            p0 = h1 * SSM_HEAD_DIM
            atts = []
            for h in (h1, h1 + 1):
                seg = colb[:, h * LANES:(h + 1) * LANES] - acst_scr[h:h + 1, :]
                atts.append((cb * jnp.where(tril, jnp.exp(seg), 0.0)).astype(BF16))
            blk = xdt_b[:, p0:p0 + pair]
            zero = jnp.zeros_like(blk)
            rhs = jnp.concatenate([jnp.where(first_head, blk, zero), jnp.where(first_head, zero, blk)], axis=0)
            y_diag = jnp.dot(jnp.concatenate(atts, axis=1), rhs, preferred_element_type=F32)
            y_scr[:, p0:p0 + pair] = (y_diag + y_off[:, j * pair:(j + 1) * pair] * e_acsx[:, p0:p0 + pair]
                                      + skip[:, p0:p0 + pair])
        st = lax.dot_general(xdte_b[:, g0:g0 + SSM_GROUP_WIDTH], bm, TN_DIMS, preferred_element_type=F32)
        for r in range(SSM_HEADS_PER_GROUP):
            h = g * SSM_HEADS_PER_GROUP + r
            r0 = r * SSM_HEAD_DIM
            h_scr[g0 + r0:g0 + r0 + SSM_HEAD_DIM, :] = (h_grp[r0:r0 + SSM_HEAD_DIM, :] * c_col[h:h + 1, :]
                                                        + st[r0:r0 + SSM_HEAD_DIM, :])


def _mixer_body(*refs, lc, lv, has_state, stack_prev):
    refs = list(refs)
    xbc_ref, dt_ref, z_ref, u_ref, v_ref = refs[:5]
    i = 5
    if has_state:
        cprev_ref, h0_ref = refs[i:i + 2]
        i += 2
    if stack_prev:
        hprev_ref = refs[i]
        i += 1
    cw_ref, cb_ref, dtb_ref, alog_ref, dsk_ref, nw_ref, ws_ref, bst_ref, ex_ref, ecol_ref = refs[i:i + 10]
    y_ref, sg_ref, hout_ref, cout_ref = refs[i + 10:i + 14]
    xp_scr, h_scr, y_scr, acst_scr = refs[i + 14:]
    c = pl.program_id(1)
    hist = SSM_CONV - 1
    base = SUBLANES

    @pl.when(c == 0)
    def _():
        if has_state:
            xp_scr[base - hist:base, :] = cprev_ref[...]
            h_scr[...] = h0_ref[...]
        else:
            xp_scr[base - hist:base, :] = jnp.zeros((hist, SSM_CONV_DIM), F32)
            h_scr[...] = jnp.zeros(h_scr.shape, F32)

    xp_scr[base:base + lc, :] = xbc_ref[...]
    xc = cb_ref[...]
    for k in range(SSM_CONV):
        xc = xc + cw_ref[k:k + 1, :] * xp_scr[base - hist + k:base - hist + k + lc, :]
    xc = _silu(xc)
    cout_ref[...] = xp_scr[base + lv - hist:base + lv, :]
    xp_scr[base - hist:base, :] = xp_scr[base + lc - hist:base + lc, :]

    row_i = lax.broadcasted_iota(jnp.int32, (lc, lc), 0)
    col_i = lax.broadcasted_iota(jnp.int32, (lc, lc), 1)
    tril = row_i >= col_i

    dt = jax.nn.softplus(dt_ref[...] + dtb_ref[...])
    if lv < lc:
        dt = jnp.where(lax.broadcasted_iota(jnp.int32, (lc, LANES), 0) < lv, dt, 0.0)
    a = -jnp.exp(alog_ref[...])
    acs = jnp.dot(tril.astype(F32), dt * a, precision=HIGHEST, preferred_element_type=F32)
    if lc == LANES:
        _ssd_wide(xc, dt, acs, tril, dsk_ref, ex_ref, ecol_ref, h_scr, y_scr, acst_scr)
    else:
        _ssd_by_head(xc, dt, acs, acs.T, tril, dsk_ref, h_scr, y_scr)

    @pl.when(c == pl.num_programs(1) - 1)
    def _():
        if stack_prev:
            hout_ref[0] = hprev_ref[...]
            hout_ref[1] = h_scr[...]
        else:
            hout_ref[...] = h_scr[...]

    yz = y_scr[...] * _silu(z_ref[...])
    for g in range(SSM_GROUPS):
        s0 = g * SSM_GROUP_WIDTH
        part = yz[:, s0:s0 + SSM_GROUP_WIDTH]
        ms = jnp.mean(part * part, axis=-1, keepdims=True)
        y_ref[:, s0:s0 + SSM_GROUP_WIDTH] = part * lax.rsqrt(ms + NORM_EPS) * nw_ref[:, s0:s0 + SSM_GROUP_WIDTH]

    u = u_ref[...]
    v = v_ref[...]
    for g in range(GM_GROUPS):
        k0 = g * GM_GROUP_DIM
        wc = jnp.where(tril, ws_ref[g, 0:lc, 0:lc], 0.0)
        s = _bdot(wc, v[:, k0:k0 + GM_GROUP_DIM]) + bst_ref[0:lc, g:g + 1]
        sg_ref[:, k0:k0 + GM_GROUP_DIM] = u[:, k0:k0 + GM_GROUP_DIM] * s


def _mixer(z, xbc, dt, u, v, w, bn, seq, lc, lv, state, h_prev_layer):
    nc = seq // lc
    has_state = state is not None
    stack_prev = h_prev_layer is not None
    hp = SSM_HEADS * SSM_HEAD_DIM
    r3 = lambda a: a.reshape(bn, seq, a.shape[-1])
    blk = lambda n: pl.BlockSpec((None, lc, n), lambda b, c: (b, c, 0))
    in_specs = [blk(SSM_CONV_DIM), blk(LANES), blk(SSM_D_INNER), blk(GM_WIDTH), blk(GM_WIDTH)]
    args = [r3(xbc), r3(dt), r3(z), r3(u), r3(v)]
    if has_state:
        s_ssm, s_conv, layer = state
        in_specs += [pl.BlockSpec((None, None, SSM_CONV - 1, SSM_CONV_DIM), lambda b, c: (layer, b, 0, 0)),
                     pl.BlockSpec((None, None, hp, SSM_STATE), lambda b, c: (layer, b, 0, 0))]
        args += [s_conv, s_ssm]
    if stack_prev:
        in_specs.append(pl.BlockSpec((None, hp, SSM_STATE), lambda b, c: (b, 0, 0)))
        args.append(h_prev_layer)
        h_spec = pl.BlockSpec((2, None, hp, SSM_STATE), lambda b, c: (0, b, 0, 0))
        h_shape = jax.ShapeDtypeStruct((2, bn, hp, SSM_STATE), F32)
    else:
        h_spec = pl.BlockSpec((None, hp, SSM_STATE), lambda b, c: (b, 0, 0))
        h_shape = jax.ShapeDtypeStruct((bn, hp, SSM_STATE), F32)
    in_specs += [_full((SSM_CONV, SSM_CONV_DIM)), _full((1, SSM_CONV_DIM)), _full((1, LANES)), _full((1, LANES)),
                 _full((1, SSM_D_INNER)), _full((1, SSM_D_INNER)), _full((GM_GROUPS, 128, 128)), _full((128, GM_GROUPS)),
                 _full(w["expand_ch"].shape), _full(w["expand_col"].shape)]
    args += [w["conv_w"], w["conv_b"], w["dt_bias"], w["a_log"], w["d_skip"], w["ssm_norm"], w["gm_ws"], w["gm_bs_t"],
             w["expand_ch"], w["expand_col"]]
    y, sg, h_new, conv_new = pl.pallas_call(
        functools.partial(_mixer_body, lc=lc, lv=lv, has_state=has_state, stack_prev=stack_prev),
        grid=(bn, nc),
        in_specs=in_specs,
        out_specs=[blk(SSM_D_INNER), blk(GM_WIDTH), h_spec,
                   pl.BlockSpec((None, SSM_CONV - 1, SSM_CONV_DIM), lambda b, c: (b, 0, 0))],
        out_shape=[jax.ShapeDtypeStruct((bn, seq, SSM_D_INNER), F32), jax.ShapeDtypeStruct((bn, seq, GM_WIDTH), F32),
                   h_shape, jax.ShapeDtypeStruct((bn, SSM_CONV - 1, SSM_CONV_DIM), F32)],
        scratch_shapes=[pltpu.VMEM((SUBLANES + lc, SSM_CONV_DIM), F32),
                        pltpu.VMEM((hp, SSM_STATE), F32),
                        pltpu.VMEM((lc, SSM_D_INNER), F32),
                        pltpu.VMEM((LANES, lc), F32)],
        compiler_params=_params(2),
        name="mixer",
    )(*args)
    t = bn * seq
    return y.reshape(t, SSM_D_INNER), sg.reshape(t, GM_WIDTH), h_new, conv_new


def _merge_body(x_ref, y_ref, sg_ref, gmix_ref, wg_ref, bg_ref, wa_ref, wb_ref, wo_ref, gmem_ref, wq_ref,
                x1_ref, q_ref):
    x = x_ref[...]
    xn = _rms(x, gmix_ref[...]).astype(BF16)
    gates = jax.nn.sigmoid(jnp.dot(xn, wg_ref[...], preferred_element_type=F32) + bg_ref[...])
    a_out = _bdot(y_ref[...], wa_ref[...])
    b_out = _bdot(sg_ref[...], wb_ref[...])
    merged = gates[:, :D_MODEL] * a_out + gates[:, D_MODEL:] * b_out
    x1 = x + _bdot(merged, wo_ref[...])
    x1_ref[...] = x1
    q_ref[...] = _bdot(_rms(x1, gmem_ref[...]), wq_ref[...]).astype(BF16)


def _merge(x, y, sg, w, tm):
    t = x.shape[0]
    row = lambda n: pl.BlockSpec((tm, n), lambda i: (i, 0))
    return pl.pallas_call(
        _merge_body,
        grid=(t // tm,),
        in_specs=[row(D_MODEL), row(SSM_D_INNER), row(GM_WIDTH), _full((1, D_MODEL)),
                  _full((D_MODEL, 2 * D_MODEL)), _full((1, 2 * D_MODEL)), _full((SSM_D_INNER, D_MODEL)),
                  _full((GM_WIDTH, D_MODEL)), _full((D_MODEL, D_MODEL)), _full((1, D_MODEL)), _full((D_MODEL, D_MODEL))],
        out_specs=[row(D_MODEL), row(D_MODEL)],
        out_shape=[jax.ShapeDtypeStruct((t, D_MODEL), F32), jax.ShapeDtypeStruct((t, D_MODEL), BF16)],
        compiler_params=_params(1),
        name="merge",
    )(x, y, sg, w["norm_mix"], w["w_gate"], w["b_gate"], w["w_a_proj"], w["w_b_proj"], w["w_out"],
      w["norm_mem"], w["w_mem_q"])


def _memkv_body(m_ref, g_ref, wk_ref, wv_ref, k_ref, v_ref, kb_ref, vb_ref):
    mn = _rms(m_ref[...], g_ref[...]).astype(BF16)
    k = jnp.dot(mn, wk_ref[...], preferred_element_type=F32)
    v = jnp.dot(mn, wv_ref[...], preferred_element_type=F32)
    for h in range(MEM_HEADS):
        lo = h * MEM_HEAD_DIM
        k_ref[:, h, :] = k[:, lo:lo + MEM_HEAD_DIM]
        v_ref[:, h, :] = v[:, lo:lo + MEM_HEAD_DIM]
        kb_ref[h] = k[:, lo:lo + MEM_HEAD_DIM].astype(BF16)
        vb_ref[h] = v[:, lo:lo + MEM_HEAD_DIM].astype(BF16)


def _memkv(mem, w):
    bn, m, _ = mem.shape
    kv_spec = pl.BlockSpec((None, m, MEM_HEADS, MEM_HEAD_DIM), lambda b: (b, 0, 0, 0))
    kv_shape = jax.ShapeDtypeStruct((bn, m, MEM_HEADS, MEM_HEAD_DIM), F32)
    hb_spec = pl.BlockSpec((None, MEM_HEADS, m, MEM_HEAD_DIM), lambda b: (b, 0, 0, 0))
    hb_shape = jax.ShapeDtypeStruct((bn, MEM_HEADS, m, MEM_HEAD_DIM), BF16)
    return pl.pallas_call(
        _memkv_body,
        grid=(bn,),
        in_specs=[pl.BlockSpec((None, m, D_MODEL), lambda b: (b, 0, 0)), _full((1, D_MODEL)),
                  _full((D_MODEL, D_MODEL)), _full((D_MODEL, D_MODEL))],
        out_specs=[kv_spec, kv_spec, hb_spec, hb_spec],
        out_shape=[kv_shape, kv_shape, hb_shape, hb_shape],
        compiler_params=_params(1),
        name="memkv",
    )(mem, w["norm_memkv"], w["w_mem_k"], w["w_mem_v"])


def _attend(q_ref, k_heads, v_heads, x_ref, wo_ref, out_ref, o_scr):
    scale = MEM_HEAD_DIM ** -0.5
    for h in range(MEM_HEADS):
        lo = h * MEM_HEAD_DIM
        s = lax.dot_general(q_ref[:, lo:lo + MEM_HEAD_DIM], k_heads[h], NT_DIMS, preferred_element_type=F32) * scale
        e = jnp.exp(s - jnp.max(s, axis=-1, keepdims=True))
        p = e / jnp.sum(e, axis=-1, keepdims=True)
        o_scr[:, lo:lo + MEM_HEAD_DIM] = jnp.dot(p.astype(BF16), v_heads[h], preferred_element_type=F32)
    out_ref[...] = x_ref[...] + _bdot(o_scr[...], wo_ref[...])


def _attn_prompt_body(q_ref, kb_ref, vb_ref, x_ref, wo_ref, out_ref, o_scr):
    _attend(q_ref, [kb_ref[h] for h in range(MEM_HEADS)], [vb_ref[h] for h in range(MEM_HEADS)],
            x_ref, wo_ref, out_ref, o_scr)


def _attn_prompt(q, kb, vb, x1, w, bn, seq, lq):
    m = kb.shape[2]
    r3 = lambda a: a.reshape(bn, seq, D_MODEL)
    blk = pl.BlockSpec((None, lq, D_MODEL), lambda b, i: (b, i, 0))
    kv_spec = pl.BlockSpec((None, MEM_HEADS, m, MEM_HEAD_DIM), lambda b, i: (b, 0, 0, 0))
    out = pl.pallas_call(
        _attn_prompt_body,
        grid=(bn, seq // lq),
        in_specs=[blk, kv_spec, kv_spec, blk, _full((D_MODEL, D_MODEL))],
        out_specs=blk,
        out_shape=jax.ShapeDtypeStruct((bn, seq, D_MODEL), F32),
        scratch_shapes=[pltpu.VMEM((lq, D_MODEL), F32)],
        compiler_params=_params(2),
        name="attn_prompt",
    )(r3(q), kb, vb, r3(x1), w["w_mem_o"])
    return out.reshape(bn * seq, D_MODEL)


def _attn_sample_body(q_ref, k_hbm, v_hbm, x_ref, wo_ref, out_ref, kbuf, vbuf, sem, o_scr, *, layer):
    b = pl.program_id(0)
    slot = b % 2

    def copies(seq_i, s):
        out = []
        for h in range(MEM_HEADS):
            out.append(pltpu.make_async_copy(k_hbm.at[layer, seq_i, :, h, :], kbuf.at[s, h], sem.at[0, s, h]))
            out.append(pltpu.make_async_copy(v_hbm.at[layer, seq_i, :, h, :], vbuf.at[s, h], sem.at[1, s, h]))
        return out

    @pl.when(b == 0)
    def _():
        for cp in copies(0, 0):
            cp.start()

    @pl.when(b + 1 < pl.num_programs(0))
    def _():
        for cp in copies(b + 1, 1 - slot):
            cp.start()

    for cp in copies(b, slot):
        cp.wait()
    _attend(q_ref, [kbuf[slot, h].astype(BF16) for h in range(MEM_HEADS)],
            [vbuf[slot, h].astype(BF16) for h in range(MEM_HEADS)], x_ref, wo_ref, out_ref, o_scr)


def _attn_sample(q, k5, v5, layer, x1, w, bn, rows):
    m = k5.shape[2]
    r3 = lambda a: a.reshape(bn, rows, D_MODEL)
    blk = pl.BlockSpec((None, rows, D_MODEL), lambda b: (b, 0, 0))
    hbm = pl.BlockSpec(memory_space=pl.ANY)
    out = pl.pallas_call(
        functools.partial(_attn_sample_body, layer=layer),
        grid=(bn,),
        in_specs=[blk, hbm, hbm, blk, _full((D_MODEL, D_MODEL))],
        out_specs=blk,
        out_shape=jax.ShapeDtypeStruct((bn, rows, D_MODEL), F32),
        scratch_shapes=[pltpu.VMEM((2, MEM_HEADS, m, MEM_HEAD_DIM), F32), pltpu.VMEM((2, MEM_HEADS, m, MEM_HEAD_DIM), F32),
                        pltpu.SemaphoreType.DMA((2, 2, MEM_HEADS)), pltpu.VMEM((rows, D_MODEL), F32)],
        compiler_params=_params(1),
        name="attn_sample",
    )(r3(q), k5, v5, r3(x1), w["w_mem_o"])
    return out.reshape(bn * rows, D_MODEL)


def _first_max(vals):
    m = functools.reduce(jnp.maximum, vals)
    idx = jnp.full(m.shape, len(vals) - 1, jnp.int32)
    for j in range(len(vals) - 2, -1, -1):
        idx = jnp.where(vals[j] == m, j, idx)
    return m, idx


def _route_body(x_ref, g_ref, wr_ref, br_ref, xn_ref, eid_ref, rank_ref, wt_ref, cnt_ref, carry_scr, *, tm):
    @pl.when(pl.program_id(0) == 0)
    def _():
        carry_scr[...] = jnp.zeros(carry_scr.shape, F32)

    xn = _rms(x_ref[...], g_ref[...])
    xn_ref[...] = xn
    lg = lax.dot_general(wr_ref[...], xn, NT_DIMS, precision=HIGHEST, preferred_element_type=F32) + br_ref[...]
    grp = [lg[j:j + 1, :] for j in range(MOE_GROUPS)]
    gmax, gid = _first_max(grp)
    gw = 1.0 / functools.reduce(jnp.add, [jnp.exp(r - gmax) for r in grp])
    ex = [lg[SUBLANES + j:SUBLANES + j + 1, :] for j in range(MOE_EXPERTS)]
    sel = []
    for j in range(MOE_EXPERTS_PER_GROUP):
        pick = ex[(MOE_GROUPS - 1) * MOE_EXPERTS_PER_GROUP + j]
        for g in range(MOE_GROUPS - 2, -1, -1):
            pick = jnp.where(gid == g, ex[g * MOE_EXPERTS_PER_GROUP + j], pick)
        sel.append(pick)
    v1, i1 = _first_max(sel)
    rest = [jnp.where(i1 == j, -jnp.inf, sel[j]) for j in range(MOE_EXPERTS_PER_GROUP)]
    v2, i2 = _first_max(rest)
    e2 = jnp.exp(v2 - v1)
    den = 1.0 + e2
    eid1 = gid * MOE_EXPERTS_PER_GROUP + i1
    eid2 = gid * MOE_EXPERTS_PER_GROUP + i2

    e_iota = lax.broadcasted_iota(jnp.int32, (MOE_EXPERTS, tm), 0)
    m1 = e_iota == eid1
    m2 = e_iota == eid2
    onehot = jnp.where(m1, 1.0, 0.0) + jnp.where(m2, 1.0, 0.0)
    before = lax.broadcasted_iota(jnp.int32, (tm, tm), 0) < lax.broadcasted_iota(jnp.int32, (tm, tm), 1)
    ranks = _bdot(onehot, jnp.where(before, 1.0, 0.0)) + carry_scr[:, 0:1]
    r1 = jnp.sum(jnp.where(m1, ranks, 0.0), axis=0, keepdims=True)
    r2 = jnp.sum(jnp.where(m2, ranks, 0.0), axis=0, keepdims=True)
    carry_scr[...] = carry_scr[...] + jnp.sum(onehot, axis=1, keepdims=True)
    cnt_ref[...] = carry_scr[...]

    zeros_i = jnp.zeros((SUBLANES - MOE_TOP_K, tm), jnp.int32)
    eid_ref[0:1, :] = eid1
    eid_ref[1:2, :] = eid2
    eid_ref[MOE_TOP_K:, :] = zeros_i
    rank_ref[0:1, :] = r1.astype(jnp.int32)
    rank_ref[1:2, :] = r2.astype(jnp.int32)
    rank_ref[MOE_TOP_K:, :] = zeros_i
    wt_ref[0:1, :] = gw / den
    wt_ref[1:2, :] = gw * e2 / den
    wt_ref[MOE_TOP_K:, :] = jnp.zeros((SUBLANES - MOE_TOP_K, tm), F32)


def _route(x, w, tm):
    t = x.shape[0]
    lane_blk = pl.BlockSpec((SUBLANES, tm), lambda i: (0, i))
    nr = w["w_router_t"].shape[0]
    return pl.pallas_call(
        functools.partial(_route_body, tm=tm),
        grid=(t // tm,),
        in_specs=[pl.BlockSpec((tm, D_MODEL), lambda i: (i, 0)), _full((1, D_MODEL)), _full((nr, D_MODEL)), _full((nr, 1))],
        out_specs=[pl.BlockSpec((tm, D_MODEL), lambda i: (i, 0)), lane_blk, lane_blk, lane_blk,
                   _full((MOE_EXPERTS, LANES))],
        out_shape=[jax.ShapeDtypeStruct((t, D_MODEL), F32), jax.ShapeDtypeStruct((SUBLANES, t), jnp.int32),
                   jax.ShapeDtypeStruct((SUBLANES, t), jnp.int32), jax.ShapeDtypeStruct((SUBLANES, t), F32),
                   jax.ShapeDtypeStruct((MOE_EXPERTS, LANES), F32)],
        scratch_shapes=[pltpu.VMEM((MOE_EXPERTS, LANES), F32)],
        compiler_params=_params(1),
        name="route",
    )(x, w["norm_ffn"], w["w_router_t"], w["b_router_t"])


def _row_copy(src, src_row, dst, dst_row, sem):
    return pltpu.make_async_copy(src.at[pl.ds(src_row, 1), :], dst.at[pl.ds(dst_row, 1), :], sem)


def _dispatch_body(dest_ref, xn_ref, init_ref, out_ref, sem, *, tm):
    del init_ref

    def issue(t, carry):
        for k in range(MOE_TOP_K):
            _row_copy(xn_ref, t, out_ref, dest_ref[k, t], sem).start()
        return carry

    def drain(t, carry):
        for k in range(MOE_TOP_K):
            _row_copy(xn_ref, t, out_ref, dest_ref[k, t], sem).wait()
        return carry

    lax.fori_loop(0, tm, issue, 0, unroll=DMA_UNROLL)
    lax.fori_loop(0, tm, drain, 0, unroll=DMA_UNROLL)


def _dispatch(xn, dest, n_slots, tm):
    t = xn.shape[0]
    return pl.pallas_call(
        functools.partial(_dispatch_body, tm=tm),
        grid=(t // tm,),
        in_specs=[pl.BlockSpec((MOE_TOP_K, tm), lambda i: (0, i), memory_space=pltpu.SMEM),
                  pl.BlockSpec((tm, D_MODEL), lambda i: (i, 0)),
                  pl.BlockSpec(memory_space=pl.ANY)],
        out_specs=pl.BlockSpec(memory_space=pl.ANY),
        out_shape=jax.ShapeDtypeStruct((n_slots, D_MODEL), F32),
        scratch_shapes=[pltpu.SemaphoreType.DMA(())],
        input_output_aliases={2: 0},
        compiler_params=_params(1, disable_bounds_checks=True),
        name="dispatch",
    )(dest, xn, jnp.zeros((n_slots, D_MODEL), F32))


def _ffn_body(be_ref, nu_ref, x_ref, wg_ref, wu_ref, wd_ref, o_ref, wg_scr, wu_scr, wd_scr):
    i = pl.program_id(0)
    last = nu_ref[0] - 1
    used = i <= last
    e_now = be_ref[jnp.minimum(i, last)]
    e_before = be_ref[jnp.minimum(jnp.maximum(i - 1, 0), last)]

    @pl.when(jnp.logical_or(i == 0, e_now != e_before))
    def _():
        wg_scr[...] = wg_ref[...].astype(BF16)
        wu_scr[...] = wu_ref[...].astype(BF16)
        wd_scr[...] = wd_ref[...].astype(BF16)

    @pl.when(used)
    def _():
        x = x_ref[...].astype(BF16)
        hid = (_silu(jnp.dot(x, wg_scr[...], preferred_element_type=F32))
               * jnp.dot(x, wu_scr[...], preferred_element_type=F32))
        o_ref[...] = jnp.dot(hid.astype(BF16), wd_scr[...], preferred_element_type=F32)

    @pl.when(jnp.logical_not(used))
    def _():
        o_ref[...] = jnp.zeros(o_ref.shape, F32)


def _ffn(xs, block_e, n_used, w, layer):
    n_slots = xs.shape[0]
    nb = n_slots // MOE_ROWS
    blk_i = lambda i, be, nu: (jnp.minimum(i, nu[0] - 1), 0)
    exp_i = lambda i, be, nu: (layer, be[jnp.minimum(i, nu[0] - 1)], 0, 0)
    grid_spec = pltpu.PrefetchScalarGridSpec(
        num_scalar_prefetch=2,
        grid=(nb,),
        in_specs=[pl.BlockSpec((MOE_ROWS, D_MODEL), blk_i),
                  pl.BlockSpec((None, None, D_MODEL, MOE_D_FF), exp_i),
                  pl.BlockSpec((None, None, D_MODEL, MOE_D_FF), exp_i),
                  pl.BlockSpec((None, None, MOE_D_FF, D_MODEL), exp_i)],
        out_specs=pl.BlockSpec((MOE_ROWS, D_MODEL), lambda i, be, nu: (i, 0)),
        scratch_shapes=[pltpu.VMEM((D_MODEL, MOE_D_FF), BF16), pltpu.VMEM((D_MODEL, MOE_D_FF), BF16),
                        pltpu.VMEM((MOE_D_FF, D_MODEL), BF16)],
    )
    return pl.pallas_call(
        _ffn_body,
        grid_spec=grid_spec,
        out_shape=jax.ShapeDtypeStruct((n_slots, D_MODEL), F32),
        compiler_params=_params(1),
        name="ffn",
    )(block_e, n_used, xs, w["w_expert_gate"], w["w_expert_up"], w["w_expert_down"])


def _combine_body(dest_ref, wt_ref, x_ref, ys_ref, gfin_ref, out_ref, buf0, buf1, sem, *, tm, final):
    bufs = (buf0, buf1)

    def issue(t, carry):
        for k in range(MOE_TOP_K):
            _row_copy(ys_ref, dest_ref[k, t], bufs[k], t, sem).start()
        return carry

    def drain(t, carry):
        for k in range(MOE_TOP_K):
            _row_copy(ys_ref, dest_ref[k, t], bufs[k], t, sem).wait()
        return carry

    lax.fori_loop(0, tm, issue, 0, unroll=DMA_UNROLL)
    lax.fori_loop(0, tm, drain, 0, unroll=DMA_UNROLL)
    wt = wt_ref[...].T
    out = x_ref[...] + (wt[:, 0:1] * buf0[...] + wt[:, 1:2] * buf1[...])
    if final:
        out = _rms(out, gfin_ref[...])
    out_ref[...] = out


def _combine(x, ys, dest, wt, g_final, tm, final):
    t = x.shape[0]
    return pl.pallas_call(
        functools.partial(_combine_body, tm=tm, final=final),
        grid=(t // tm,),
        in_specs=[pl.BlockSpec((MOE_TOP_K, tm), lambda i: (0, i), memory_space=pltpu.SMEM),
                  pl.BlockSpec((SUBLANES, tm), lambda i: (0, i)),
                  pl.BlockSpec((tm, D_MODEL), lambda i: (i, 0)),
                  pl.BlockSpec(memory_space=pl.ANY),
                  _full((1, D_MODEL))],
        out_specs=pl.BlockSpec((tm, D_MODEL), lambda i: (i, 0)),
        out_shape=jax.ShapeDtypeStruct((t, D_MODEL), F32),
        scratch_shapes=[pltpu.VMEM((tm, D_MODEL), F32), pltpu.VMEM((tm, D_MODEL), F32), pltpu.SemaphoreType.DMA(())],
        compiler_params=_params(1, disable_bounds_checks=True),
        name="combine",
    )(dest, wt, x, ys, g_final)


def _moe(x, w, layer, g_final, final, tm):
    t = x.shape[0]
    xn, eid, rank, wt, cnt = _route(x, w, tm)
    counts = cnt[:, 0].astype(jnp.int32)
    padded = (counts + MOE_ROWS - 1) // MOE_ROWS * MOE_ROWS
    pad_ends = jnp.cumsum(padded)
    pad_starts = pad_ends - padded
    n_blocks = (t * MOE_TOP_K) // MOE_ROWS + MOE_EXPERTS
    e_ids = jnp.arange(MOE_EXPERTS, dtype=jnp.int32)
    start_of = jnp.sum(jnp.where(eid[:MOE_TOP_K, :, None] == e_ids, pad_starts, 0), axis=-1)
    dest = start_of + rank[:MOE_TOP_K]
    block_lo = jnp.arange(n_blocks, dtype=jnp.int32) * MOE_ROWS
    block_e = jnp.minimum(jnp.sum((pad_ends[None, :] <= block_lo[:, None]).astype(jnp.int32), axis=1), MOE_EXPERTS - 1)
    n_used = (pad_ends[-1:] // MOE_ROWS).astype(jnp.int32)
    xs = _dispatch(xn, dest, n_blocks * MOE_ROWS, tm)
    ys = _ffn(xs, block_e, n_used, w, layer)
    return _combine(x, ys, dest, wt, g_final, tm, final)


def _expansion_tables():
    k = jnp.arange(SPLIT_PARTS * LANES, dtype=jnp.int32)[:, None] % LANES
    ch = jnp.arange(SSM_D_INNER, dtype=jnp.int32)[None, :] // SSM_HEAD_DIM
    col = jnp.arange(SSM_HEADS * LANES, dtype=jnp.int32)[None, :] // LANES
    return (k == ch).astype(BF16), (k == col).astype(BF16)


def _layer_weights(l, p):
    row = lambda a: a[l].reshape(1, -1)
    s1 = SSM_D_INNER
    s2 = s1 + SSM_CONV_DIM
    s3 = s2 + SSM_HEADS
    w_in = p["w_in"][l]
    pad_h = LANES - SSM_HEADS
    wr = jnp.zeros((SUBLANES + MOE_EXPERTS, D_MODEL), F32)
    wr = wr.at[:MOE_GROUPS].set(p["w_router_group"][l].T).at[SUBLANES:].set(p["w_router_expert"][l].T)
    br = jnp.zeros((SUBLANES + MOE_EXPERTS, 1), F32)
    br = br.at[:MOE_GROUPS, 0].set(p["b_router_group"][l]).at[SUBLANES:, 0].set(p["b_router_expert"][l].reshape(-1))
    expand_ch, expand_col = _expansion_tables()
    return {
        "norm_mix": row(p["norm_mix"]),
        "w_z": w_in[:, :s1].astype(BF16),
        "w_xbc": w_in[:, s1:s2].astype(BF16),
        "w_dt": jnp.pad(w_in[:, s2:s3], ((0, 0), (0, pad_h))).astype(BF16),
        "w_u": w_in[:, s3:s3 + GM_WIDTH].astype(BF16),
        "w_v": w_in[:, s3 + GM_WIDTH:].astype(BF16),
        "gm_ln_g": row(p["gm_ln_g"]), "gm_ln_b": row(p["gm_ln_b"]),
        "conv_w": p["conv_w"][l], "conv_b": row(p["conv_b"]),
        "dt_bias": jnp.pad(row(p["dt_bias"]), ((0, 0), (0, pad_h))),
        "a_log": jnp.pad(row(p["a_log"]), ((0, 0), (0, pad_h))),
        "d_skip": jnp.repeat(p["d_skip"][l], SSM_HEAD_DIM).reshape(1, -1),
        "ssm_norm": row(p["ssm_norm"]),
        "gm_ws": p["gm_ws"][l], "gm_bs_t": p["gm_bs"][l].T,
        "expand_ch": expand_ch, "expand_col": expand_col,
        "w_gate": p["w_gate"][l].astype(BF16), "b_gate": row(p["b_gate"]),
        "w_a_proj": p["w_a_proj"][l].astype(BF16), "w_b_proj": p["w_b_proj"][l].astype(BF16),
        "w_out": p["w_out"][l].astype(BF16),
        "norm_mem": row(p["norm_mem"]), "norm_memkv": row(p["norm_memkv"]),
        "w_mem_q": p["w_mem_q"][l].astype(BF16), "w_mem_k": p["w_mem_k"][l].astype(BF16),
        "w_mem_v": p["w_mem_v"][l].astype(BF16), "w_mem_o": p["w_mem_o"][l].astype(BF16),
        "norm_ffn": row(p["norm_ffn"]),
        "w_router_t": wr, "b_router_t": br,
        "w_expert_gate": p["w_expert_gate"], "w_expert_up": p["w_expert_up"], "w_expert_down": p["w_expert_down"],
    }


def kernel(x_prompt, x_sample, mem_prompt, state_ssm, state_conv, cache_mem_k, cache_mem_v, norm_mix, w_in, conv_w, conv_b, dt_bias, a_log, d_skip, ssm_norm, w_a_proj, gm_ln_g, gm_ln_b, gm_ws, gm_bs, w_b_proj, w_gate, b_gate, w_out, norm_mem, norm_memkv, w_mem_q, w_mem_k, w_mem_v, w_mem_o, norm_ffn, w_router_group, b_router_group, w_router_expert, b_router_expert, w_expert_gate, w_expert_up, w_expert_down, norm_final):
    p = dict(norm_mix=norm_mix, w_in=w_in, conv_w=conv_w, conv_b=conv_b, dt_bias=dt_bias, a_log=a_log, d_skip=d_skip,
             ssm_norm=ssm_norm, w_a_proj=w_a_proj, gm_ln_g=gm_ln_g, gm_ln_b=gm_ln_b, gm_ws=gm_ws, gm_bs=gm_bs,
             w_b_proj=w_b_proj, w_gate=w_gate, b_gate=b_gate, w_out=w_out, norm_mem=norm_mem, norm_memkv=norm_memkv,
             w_mem_q=w_mem_q, w_mem_k=w_mem_k, w_mem_v=w_mem_v, w_mem_o=w_mem_o, norm_ffn=norm_ffn,
             w_router_group=w_router_group, b_router_group=b_router_group, w_router_expert=w_router_expert,
             b_router_expert=b_router_expert, w_expert_gate=w_expert_gate, w_expert_up=w_expert_up,
             w_expert_down=w_expert_down)
    depth = w_in.shape[0]
    bp, lp, _ = x_prompt.shape
    bs, ls, _ = x_sample.shape
    assert lp % PROMPT_CHUNK == 0 and ls <= SAMPLE_ROWS
    g_final = norm_final.reshape(1, -1)
    hp = SSM_HEADS * SSM_HEAD_DIM
    state_shape = (SSM_HEADS, SSM_HEAD_DIM, SSM_STATE)
    state_ssm2 = state_ssm.reshape(depth, bs, hp, SSM_STATE)

    xp = x_prompt.reshape(bp * lp, D_MODEL)
    xs = jnp.pad(x_sample, ((0, 0), (0, SAMPLE_ROWS - ls), (0, 0))).reshape(bs * SAMPLE_ROWS, D_MODEL)
    tp, ts = bp * lp, bs * SAMPLE_ROWS
    tm_p, tm_s = min(256, tp), min(256, ts)
    tmm_p, tmm_s = min(512, tp), min(512, ts)
    lq_p = min(1024, lp)

    ssm_p, conv_p, mk_p, mv_p, ssm_s, conv_s, gv_s = [], [], [], [], [], [], []
    for l in range(depth):
        w = _layer_weights(l, p)
        final = l == depth - 1
        k_p, v_p, kb, vb = _memkv(mem_prompt, w)
        z, xbc, dt, u, v = _inproj(xp, w, tm_p)
        y, sg, h_new, c_new = _mixer(z, xbc, dt, u, v, w, bp, lp, PROMPT_CHUNK, PROMPT_CHUNK, None, None)
        x1, q = _merge(xp, y, sg, w, tm_p)
        x2 = _attn_prompt(q, kb, vb, x1, w, bp, lp, lq_p)
        xp = _moe(x2, w, l, g_final, final, tmm_p)
        ssm_p.append(h_new.reshape((bp,) + state_shape))
        conv_p.append(c_new)
        mk_p.append(k_p)
        mv_p.append(v_p)
        z, xbc, dt, u, v = _inproj(xs, w, tm_s)
        stack_two = depth == 2 and l == 1
        y, sg, h_new, c_new = _mixer(z, xbc, dt, u, v, w, bs, SAMPLE_ROWS, SAMPLE_ROWS, ls,
                                     (state_ssm2, state_conv, l), ssm_s[0] if stack_two else None)
        x1, q = _merge(xs, y, sg, w, tm_s)
        x2 = _attn_sample(q, cache_mem_k, cache_mem_v, l, x1, w, bs, SAMPLE_ROWS)
        xs = _moe(x2, w, l, g_final, final, tmm_s)
        if stack_two:
            ssm_s = h_new.reshape((depth, bs) + state_shape)
        else:
            ssm_s.append(h_new)
        conv_s.append(c_new)
        gv_s.append(v.reshape(bs, SAMPLE_ROWS, GM_WIDTH)[:, :ls])
    if isinstance(ssm_s, list):
        ssm_s = jnp.stack(ssm_s).reshape((depth, bs) + state_shape)
    y_prompt = xp.reshape(bp, lp, D_MODEL)
    y_sample = xs.reshape(bs, SAMPLE_ROWS, D_MODEL)[:, :ls]
    return (y_prompt, y_sample, jnp.stack(ssm_p), jnp.stack(conv_p), jnp.stack(mk_p), jnp.stack(mv_p),
            ssm_s, jnp.stack(conv_s), jnp.stack(gv_s))
```

```python
import functools

import jax
import jax.numpy as jnp
from jax import lax
from jax.experimental import pallas as pl
from jax.experimental.pallas import tpu as pltpu

F32 = jnp.float32
BF16 = jnp.bfloat16
HIGHEST = lax.Precision.HIGHEST

NORM_EPS = 1e-6
D_MODEL = 1024
SSM_D_INNER = 1536
SSM_HEAD_DIM = 64
SSM_HEADS = 24
SSM_GROUPS = 4
SSM_HEADS_PER_GROUP = 6
SSM_STATE = 128
SSM_CONV = 4
SSM_CONV_DIM = 2560
SSM_GROUP_WIDTH = SSM_D_INNER // SSM_GROUPS
GM_WIDTH = 512
GM_GROUPS = 4
GM_GROUP_DIM = 128
MEM_HEADS = 4
MEM_HEAD_DIM = 256
MOE_GROUPS = 4
MOE_EXPERTS_PER_GROUP = 4
MOE_EXPERTS = 16
MOE_TOP_K = 2
MOE_D_FF = 512

LANES = 128
SUBLANES = 8
SAMPLE_ROWS = 8
SAMPLE_SEQS_PER_STEP = 4
PROMPT_CHUNK = 128
MOE_ROWS_PROMPT = 512
MOE_ROWS_SAMPLE = 128
SPLIT_PARTS = 3
DMA_UNROLL = 8
VMEM_LIMIT = 48 * 1024 * 1024

NT_DIMS = (((1,), (1,)), ((), ()))
TN_DIMS = (((0,), (0,)), ((), ()))


def _params(n_axes, **kw):
    return pltpu.CompilerParams(dimension_semantics=("arbitrary",) * n_axes, vmem_limit_bytes=VMEM_LIMIT, **kw)


def _rms(x, g):
    return x * lax.rsqrt(jnp.mean(x * x, axis=-1, keepdims=True) + NORM_EPS) * g


def _silu(x):
    return x * jax.nn.sigmoid(x)


def _gelu(x):
    return 0.5 * x * (1.0 + lax.erf(x * (2.0 ** -0.5)))


def _bdot(a, b):
    return jnp.dot(a.astype(BF16), b.astype(BF16), preferred_element_type=F32)


def _full(shape):
    n = len(shape)
    return pl.BlockSpec(shape, lambda *_: (0,) * n)


def _inproj_body(x_ref, g_ref, wz_ref, wx_ref, wdt_ref, wu_ref, wv_ref, lng_ref, lnb_ref,
                 z_ref, xbc_ref, dt_ref, u_ref, v_ref):
    xn = _rms(x_ref[...], g_ref[...]).astype(BF16)
    z_ref[...] = jnp.dot(xn, wz_ref[...], preferred_element_type=F32)
    xbc_ref[...] = jnp.dot(xn, wx_ref[...], preferred_element_type=F32)
    dt_ref[...] = jnp.dot(xn, wdt_ref[...], preferred_element_type=F32)
    u_ref[...] = _gelu(jnp.dot(xn, wu_ref[...], preferred_element_type=F32))
    v = _gelu(jnp.dot(xn, wv_ref[...], preferred_element_type=F32))
    vc = v - jnp.mean(v, axis=-1, keepdims=True)
    var = jnp.mean(vc * vc, axis=-1, keepdims=True)
    v_ref[...] = vc * lax.rsqrt(var + NORM_EPS) * lng_ref[...] + lnb_ref[...]


def _inproj(x, w, tm):
    t = x.shape[0]
    row = lambda n: pl.BlockSpec((tm, n), lambda i: (i, 0))
    outs = [SSM_D_INNER, SSM_CONV_DIM, LANES, GM_WIDTH, GM_WIDTH]
    return pl.pallas_call(
        _inproj_body,
        grid=(t // tm,),
        in_specs=[row(D_MODEL), _full((1, D_MODEL)), _full((D_MODEL, SSM_D_INNER)), _full((D_MODEL, SSM_CONV_DIM)),
                  _full((D_MODEL, LANES)), _full((D_MODEL, GM_WIDTH)), _full((D_MODEL, GM_WIDTH)),
                  _full((1, GM_WIDTH)), _full((1, GM_WIDTH))],
        out_specs=[row(n) for n in outs],
        out_shape=[jax.ShapeDtypeStruct((t, n), F32) for n in outs],
        compiler_params=_params(1),
        name="inproj",
    )(x, w["norm_mix"], w["w_z"], w["w_xbc"], w["w_dt"], w["w_u"], w["w_v"], w["gm_ln_g"], w["gm_ln_b"])


CONV_HIST = SSM_CONV - 1
CONV_BASE = SUBLANES


def _split_bf16(x):
    parts = []
    rest = x
    for _ in range(SPLIT_PARTS):
        piece = rest.astype(BF16)
        parts.append(piece)
        rest = rest - piece.astype(F32)
    return jnp.concatenate(parts, axis=1)


def _conv_silu(win_ref, lc, cw_ref, cb_ref):
    xc = cb_ref[...]
    for k in range(SSM_CONV):
        lo = CONV_BASE - CONV_HIST + k
        xc = xc + cw_ref[k:k + 1, :] * win_ref[lo:lo + lc, :]
    return _silu(xc)


def _decay_cumsum(dt_raw, dtb_ref, alog_ref, tril, lv):
    lc = dt_raw.shape[0]
    dt = jax.nn.softplus(dt_raw + dtb_ref[...])
    if lv < lc:
        dt = jnp.where(lax.broadcasted_iota(jnp.int32, (lc, LANES), 0) < lv, dt, 0.0)
    a = -jnp.exp(alog_ref[...])
    acs = jnp.dot(tril.astype(F32), dt * a, precision=HIGHEST, preferred_element_type=F32)
    return dt, acs


def _gate_norm(y, z, nw_ref, out_ref):
    yz = y * _silu(z)
    for g in range(SSM_GROUPS):
        s0 = g * SSM_GROUP_WIDTH
        part = yz[:, s0:s0 + SSM_GROUP_WIDTH]
        ms = jnp.mean(part * part, axis=-1, keepdims=True)
        out_ref[:, s0:s0 + SSM_GROUP_WIDTH] = part * lax.rsqrt(ms + NORM_EPS) * nw_ref[:, s0:s0 + SSM_GROUP_WIDTH]


def _spatial_gate(u, v, tril, ws_ref, bst_ref, out_ref):
    lc = u.shape[0]
    for g in range(GM_GROUPS):
        k0 = g * GM_GROUP_DIM
        wc = jnp.where(tril, ws_ref[g, 0:lc, 0:lc], 0.0)
        s = _bdot(wc, v[:, k0:k0 + GM_GROUP_DIM]) + bst_ref[0:lc, g:g + 1]
        out_ref[:, k0:k0 + GM_GROUP_DIM] = u[:, k0:k0 + GM_GROUP_DIM] * s


def _tril(lc):
    return lax.broadcasted_iota(jnp.int32, (lc, lc), 0) >= lax.broadcasted_iota(jnp.int32, (lc, lc), 1)


def _ssd_wide(xc, dt, acs, tril, dsk_ref, ex_ref, ecol_ref, h_scr, y_scr, acst_scr):
    lc = xc.shape[0]
    acs_parts = _split_bf16(acs)
    wide = jnp.dot(jnp.concatenate([_split_bf16(dt), acs_parts], axis=0), ex_ref[...], preferred_element_type=F32)
    dtx = wide[:lc]
    acsx = wide[lc:]
    colb = jnp.dot(acs_parts, ecol_ref[...], preferred_element_type=F32)
    acst_scr[...] = acs.T
    c_col = jnp.exp(acst_scr[:, lc - 1:lc])
    xs = xc[:, :SSM_D_INNER]
    xdt = xs * dtx
    xdt_b = xdt.astype(BF16)
    xdte_b = (xdt * jnp.exp(acsx[lc - 1:lc, :] - acsx)).astype(BF16)
    e_acsx = jnp.exp(acsx)
    skip = dsk_ref[...] * xs
    first_head = lax.broadcasted_iota(jnp.int32, (lc, LANES), 1) < SSM_HEAD_DIM
    pair = 2 * SSM_HEAD_DIM
    for g in range(SSM_GROUPS):
        b0 = SSM_D_INNER + g * SSM_STATE
        c0 = SSM_D_INNER + SSM_GROUPS * SSM_STATE + g * SSM_STATE
        g0 = g * SSM_GROUP_WIDTH
        bm = xc[:, b0:b0 + SSM_STATE].astype(BF16)
        cm = xc[:, c0:c0 + SSM_STATE].astype(BF16)
        cb = lax.dot_general(cm, bm, NT_DIMS, preferred_element_type=F32)
        h_grp = h_scr[g0:g0 + SSM_GROUP_WIDTH, :]
        y_off = lax.dot_general(cm, h_grp.astype(BF16), NT_DIMS, preferred_element_type=F32)
        for j in range(SSM_HEADS_PER_GROUP // 2):
            h1 = g * SSM_HEADS_PER_GROUP + 2 * j
            p0 = h1 * SSM_HEAD_DIM
            atts = []
            for h in (h1, h1 + 1):
                seg = colb[:, h * LANES:(h + 1) * LANES] - acst_scr[h:h + 1, :]
                atts.append((cb * jnp.where(tril, jnp.exp(seg), 0.0)).astype(BF16))
            blk = xdt_b[:, p0:p0 + pair]
            zero = jnp.zeros_like(blk)
            rhs = jnp.concatenate([jnp.where(first_head, blk, zero), jnp.where(first_head, zero, blk)], axis=0)
            y_diag = jnp.dot(jnp.concatenate(atts, axis=1), rhs, preferred_element_type=F32)
            y_scr[:, p0:p0 + pair] = (y_diag + y_off[:, j * pair:(j + 1) * pair] * e_acsx[:, p0:p0 + pair]
                                      + skip[:, p0:p0 + pair])
        st = lax.dot_general(xdte_b[:, g0:g0 + SSM_GROUP_WIDTH], bm, TN_DIMS, preferred_element_type=F32)
        for r in range(SSM_HEADS_PER_GROUP):
            h = g * SSM_HEADS_PER_GROUP + r
            r0 = r * SSM_HEAD_DIM
            h_scr[g0 + r0:g0 + r0 + SSM_HEAD_DIM, :] = (h_grp[r0:r0 + SSM_HEAD_DIM, :] * c_col[h:h + 1, :]
                                                        + st[r0:r0 + SSM_HEAD_DIM, :])


def _mixer_prompt_body(xbc_ref, dt_ref, z_ref, u_ref, v_ref, cw_ref, cb_ref, dtb_ref, alog_ref, dsk_ref, nw_ref,
                       ws_ref, bst_ref, ex_ref, ecol_ref, y_ref, sg_ref, hout_ref, cout_ref,
                       xp_scr, h_scr, y_scr, acst_scr):
    lc = PROMPT_CHUNK
    c = pl.program_id(1)

    @pl.when(c == 0)
    def _():
        xp_scr[CONV_BASE - CONV_HIST:CONV_BASE, :] = jnp.zeros((CONV_HIST, SSM_CONV_DIM), F32)
        h_scr[...] = jnp.zeros(h_scr.shape, F32)

    xp_scr[CONV_BASE:CONV_BASE + lc, :] = xbc_ref[...]
    xc = _conv_silu(xp_scr, lc, cw_ref, cb_ref)
    tail = xp_scr[CONV_BASE + lc - CONV_HIST:CONV_BASE + lc, :]
    cout_ref[...] = tail
    xp_scr[CONV_BASE - CONV_HIST:CONV_BASE, :] = tail

    tril = _tril(lc)
    dt, acs = _decay_cumsum(dt_ref[...], dtb_ref, alog_ref, tril, lc)
    _ssd_wide(xc, dt, acs, tril, dsk_ref, ex_ref, ecol_ref, h_scr, y_scr, acst_scr)

    @pl.when(c == pl.num_programs(1) - 1)
    def _():
        hout_ref[...] = h_scr[...]

    _gate_norm(y_scr[...], z_ref[...], nw_ref, y_ref)
    _spatial_gate(u_ref[...], v_ref[...], tril, ws_ref, bst_ref, sg_ref)


def _mixer_weight_specs(w, names):
    return [_full(w[n].shape) for n in names], [w[n] for n in names]


def _mixer_prompt(z, xbc, dt, u, v, w, bn, seq):
    lc = PROMPT_CHUNK
    hp = SSM_HEADS * SSM_HEAD_DIM
    r3 = lambda a: a.reshape(bn, seq, a.shape[-1])
    blk = lambda n: pl.BlockSpec((None, lc, n), lambda b, c: (b, c, 0))
    w_specs, w_args = _mixer_weight_specs(w, ["conv_w", "conv_b", "dt_bias", "a_log", "d_skip", "ssm_norm", "gm_ws",
                                              "gm_bs_t", "expand_ch", "expand_col"])
    y, sg, h_new, conv_new = pl.pallas_call(
        _mixer_prompt_body,
        grid=(bn, seq // lc),
        in_specs=[blk(SSM_CONV_DIM), blk(LANES), blk(SSM_D_INNER), blk(GM_WIDTH), blk(GM_WIDTH)] + w_specs,
        out_specs=[blk(SSM_D_INNER), blk(GM_WIDTH),
                   pl.BlockSpec((None, hp, SSM_STATE), lambda b, c: (b, 0, 0)),
                   pl.BlockSpec((None, CONV_HIST, SSM_CONV_DIM), lambda b, c: (b, 0, 0))],
        out_shape=[jax.ShapeDtypeStruct((bn, seq, SSM_D_INNER), F32), jax.ShapeDtypeStruct((bn, seq, GM_WIDTH), F32),
                   jax.ShapeDtypeStruct((bn, hp, SSM_STATE), F32),
                   jax.ShapeDtypeStruct((bn, CONV_HIST, SSM_CONV_DIM), F32)],
        scratch_shapes=[pltpu.VMEM((CONV_BASE + lc, SSM_CONV_DIM), F32),
                        pltpu.VMEM((hp, SSM_STATE), F32),
                        pltpu.VMEM((lc, SSM_D_INNER), F32),
                        pltpu.VMEM((LANES, lc), F32)],
        compiler_params=_params(2),
        name="mixer_prompt",
    )(r3(xbc), r3(dt), r3(z), r3(u), r3(v), *w_args)
    t = bn * seq
    return y.reshape(t, SSM_D_INNER), sg.reshape(t, GM_WIDTH), h_new, conv_new


def _ssd_short(xc, dt, acs, lv, dsk_ref, ex_ref, h_ref, hout_ref):
    lc = xc.shape[0]
    tot = acs[lc - 1:lc, :]
    e_acs = jnp.exp(acs)
    dt_end = dt * jnp.exp(tot - acs)
    c_col = jnp.exp(acs.T[:, lc - 1:lc])
    row = lax.broadcasted_iota(jnp.int32, (lc, LANES), 0)
    head = lax.broadcasted_iota(jnp.int32, (lc, LANES), 1)
    bms, cms, cbs = [], [], []
    for g in range(SSM_GROUPS):
        b0 = SSM_D_INNER + g * SSM_STATE
        c0 = SSM_D_INNER + SSM_GROUPS * SSM_STATE + g * SSM_STATE
        bms.append(xc[:, b0:b0 + SSM_STATE].astype(BF16))
        cms.append(xc[:, c0:c0 + SSM_STATE].astype(BF16))
        cbs.append(lax.dot_general(cms[g], bms[g], NT_DIMS, preferred_element_type=F32))
    coef = []
    for s in range(lv):
        decay = jnp.where(row >= s, jnp.exp(acs - acs[s:s + 1, :]), 0.0) * dt[s:s + 1, :]
        cb_s = jnp.zeros((lc, LANES), F32)
        for g in range(SSM_GROUPS):
            in_group = (head >= g * SSM_HEADS_PER_GROUP) & (head < (g + 1) * SSM_HEADS_PER_GROUP)
            cb_s = jnp.where(in_group, cbs[g][:, s:s + 1], cb_s)
        coef.append(decay * cb_s)
    wide = jnp.dot(_split_bf16(jnp.concatenate(coef + [e_acs, dt_end], axis=0)), ex_ref[...],
                   preferred_element_type=F32)
    xs = xc[:, :SSM_D_INNER]
    y = dsk_ref[...] * xs
    for s in range(lv):
        y = y + wide[s * lc:(s + 1) * lc] * xs[s:s + 1, :]
    e_acsx = wide[lv * lc:(lv + 1) * lc]
    xdte_b = (xs * wide[(lv + 1) * lc:]).astype(BF16)
    y_off = []
    for g in range(SSM_GROUPS):
        g0 = g * SSM_GROUP_WIDTH
        h_grp = h_ref[g0:g0 + SSM_GROUP_WIDTH, :]
        y_off.append(lax.dot_general(cms[g], h_grp.astype(BF16), NT_DIMS, preferred_element_type=F32))
        st = lax.dot_general(xdte_b[:, g0:g0 + SSM_GROUP_WIDTH], bms[g], TN_DIMS, preferred_element_type=F32)
        for r in range(SSM_HEADS_PER_GROUP):
            h = g * SSM_HEADS_PER_GROUP + r
            r0 = r * SSM_HEAD_DIM
            hout_ref[g0 + r0:g0 + r0 + SSM_HEAD_DIM, :] = (h_grp[r0:r0 + SSM_HEAD_DIM, :] * c_col[h:h + 1, :]
                                                           + st[r0:r0 + SSM_HEAD_DIM, :])
    return y + jnp.concatenate(y_off, axis=1) * e_acsx


def _mixer_sample_body(*refs, lv, sb, stack_prev):
    xbc_ref, dt_ref, z_ref, u_ref, v_ref, cprev_ref, h0_ref = refs[:7]
    i = 7
    if stack_prev:
        hprev_ref = refs[i]
        i += 1
    cw_ref, cb_ref, dtb_ref, alog_ref, dsk_ref, nw_ref, ws_ref, bst_ref, ex_ref = refs[i:i + 9]
    y_ref, sg_ref, hout_ref, cout_ref, xp_scr = refs[i + 9:]
    lc = SAMPLE_ROWS
    tril = _tril(lc)
    for s in range(sb):
        win = xp_scr.at[s]
        win[CONV_BASE - CONV_HIST:CONV_BASE, :] = cprev_ref[s]
        win[CONV_BASE:CONV_BASE + lc, :] = xbc_ref[s]
        xc = _conv_silu(win, lc, cw_ref, cb_ref)
        cout_ref[s] = win[CONV_BASE + lv - CONV_HIST:CONV_BASE + lv, :]
        dt, acs = _decay_cumsum(dt_ref[s], dtb_ref, alog_ref, tril, lv)
        if stack_prev:
            hout_ref[0, s] = hprev_ref[s]
            h_out = hout_ref.at[1, s]
        else:
            h_out = hout_ref.at[s]
        y = _ssd_short(xc, dt, acs, lv, dsk_ref, ex_ref, h0_ref.at[s], h_out)
        _gate_norm(y, z_ref[s], nw_ref, y_ref.at[s])
        _spatial_gate(u_ref[s], v_ref[s], tril, ws_ref, bst_ref, sg_ref.at[s])


def _mixer_sample(z, xbc, dt, u, v, w, bn, lv, state_ssm, state_conv, layer, h_prev_layer):
    lc = SAMPLE_ROWS
    sb = SAMPLE_SEQS_PER_STEP if bn % SAMPLE_SEQS_PER_STEP == 0 else 1
    stack_prev = h_prev_layer is not None
    hp = SSM_HEADS * SSM_HEAD_DIM
    r3 = lambda a: a.reshape(bn, lc, a.shape[-1])
    blk = lambda n: pl.BlockSpec((sb, lc, n), lambda b: (b, 0, 0))
    in_specs = [blk(SSM_CONV_DIM), blk(LANES), blk(SSM_D_INNER), blk(GM_WIDTH), blk(GM_WIDTH),
                pl.BlockSpec((None, sb, CONV_HIST, SSM_CONV_DIM), lambda b: (layer, b, 0, 0)),
                pl.BlockSpec((None, sb, hp, SSM_STATE), lambda b: (layer, b, 0, 0))]
    args = [r3(xbc), r3(dt), r3(z), r3(u), r3(v), state_conv, state_ssm]
    if stack_prev:
        in_specs.append(pl.BlockSpec((sb, hp, SSM_STATE), lambda b: (b, 0, 0)))
        args.append(h_prev_layer)
        h_spec = pl.BlockSpec((2, sb, hp, SSM_STATE), lambda b: (0, b, 0, 0))
        h_shape = jax.ShapeDtypeStruct((2, bn, hp, SSM_STATE), F32)
    else:
        h_spec = pl.BlockSpec((sb, hp, SSM_STATE), lambda b: (b, 0, 0))
        h_shape = jax.ShapeDtypeStruct((bn, hp, SSM_STATE), F32)
    w_specs, w_args = _mixer_weight_specs(w, ["conv_w", "conv_b", "dt_bias", "a_log", "d_skip", "ssm_norm", "gm_ws",
                                              "gm_bs_t", "expand_ch"])
    y, sg, h_new, conv_new = pl.pallas_call(
        functools.partial(_mixer_sample_body, lv=lv, sb=sb, stack_prev=stack_prev),
        grid=(bn // sb,),
        in_specs=in_specs + w_specs,
        out_specs=[blk(SSM_D_INNER), blk(GM_WIDTH), h_spec,
                   pl.BlockSpec((sb, CONV_HIST, SSM_CONV_DIM), lambda b: (b, 0, 0))],
        out_shape=[jax.ShapeDtypeStruct((bn, lc, SSM_D_INNER), F32), jax.ShapeDtypeStruct((bn, lc, GM_WIDTH), F32),
                   h_shape, jax.ShapeDtypeStruct((bn, CONV_HIST, SSM_CONV_DIM), F32)],
        scratch_shapes=[pltpu.VMEM((sb, CONV_BASE + lc, SSM_CONV_DIM), F32)],
        compiler_params=_params(1),
        name="mixer_sample",
    )(*args, *w_args)
    t = bn * lc
    return y.reshape(t, SSM_D_INNER), sg.reshape(t, GM_WIDTH), h_new, conv_new


def _merge_body(x_ref, y_ref, sg_ref, gmix_ref, wg_ref, bg_ref, wa_ref, wb_ref, wo_ref, gmem_ref, wq_ref,
                x1_ref, q_ref):
    x = x_ref[...]
    xn = _rms(x, gmix_ref[...]).astype(BF16)
    gates = jax.nn.sigmoid(jnp.dot(xn, wg_ref[...], preferred_element_type=F32) + bg_ref[...])
    a_out = _bdot(y_ref[...], wa_ref[...])
    b_out = _bdot(sg_ref[...], wb_ref[...])
    merged = gates[:, :D_MODEL] * a_out + gates[:, D_MODEL:] * b_out
    x1 = x + _bdot(merged, wo_ref[...])
    x1_ref[...] = x1
    q_ref[...] = _bdot(_rms(x1, gmem_ref[...]), wq_ref[...]).astype(BF16)


def _merge(x, y, sg, w, tm):
    t = x.shape[0]
    row = lambda n: pl.BlockSpec((tm, n), lambda i: (i, 0))
    return pl.pallas_call(
        _merge_body,
        grid=(t // tm,),
        in_specs=[row(D_MODEL), row(SSM_D_INNER), row(GM_WIDTH), _full((1, D_MODEL)),
                  _full((D_MODEL, 2 * D_MODEL)), _full((1, 2 * D_MODEL)), _full((SSM_D_INNER, D_MODEL)),
                  _full((GM_WIDTH, D_MODEL)), _full((D_MODEL, D_MODEL)), _full((1, D_MODEL)), _full((D_MODEL, D_MODEL))],
        out_specs=[row(D_MODEL), row(D_MODEL)],
        out_shape=[jax.ShapeDtypeStruct((t, D_MODEL), F32), jax.ShapeDtypeStruct((t, D_MODEL), BF16)],
        compiler_params=_params(1),
        name="merge",
    )(x, y, sg, w["norm_mix"], w["w_gate"], w["b_gate"], w["w_a_proj"], w["w_b_proj"], w["w_out"],
      w["norm_mem"], w["w_mem_q"])


def _memkv_body(m_ref, g_ref, wk_ref, wv_ref, k_ref, v_ref, kb_ref, vb_ref):
    mn = _rms(m_ref[...], g_ref[...]).astype(BF16)
    k = jnp.dot(mn, wk_ref[...], preferred_element_type=F32)
    v = jnp.dot(mn, wv_ref[...], preferred_element_type=F32)
    for h in range(MEM_HEADS):
        lo = h * MEM_HEAD_DIM
        k_ref[:, h, :] = k[:, lo:lo + MEM_HEAD_DIM]
        v_ref[:, h, :] = v[:, lo:lo + MEM_HEAD_DIM]
        kb_ref[h] = k[:, lo:lo + MEM_HEAD_DIM].astype(BF16)
        vb_ref[h] = v[:, lo:lo + MEM_HEAD_DIM].astype(BF16)


def _memkv(mem, w):
    bn, m, _ = mem.shape
    kv_spec = pl.BlockSpec((None, m, MEM_HEADS, MEM_HEAD_DIM), lambda b: (b, 0, 0, 0))
    kv_shape = jax.ShapeDtypeStruct((bn, m, MEM_HEADS, MEM_HEAD_DIM), F32)
    hb_spec = pl.BlockSpec((None, MEM_HEADS, m, MEM_HEAD_DIM), lambda b: (b, 0, 0, 0))
    hb_shape = jax.ShapeDtypeStruct((bn, MEM_HEADS, m, MEM_HEAD_DIM), BF16)
    return pl.pallas_call(
        _memkv_body,
        grid=(bn,),
        in_specs=[pl.BlockSpec((None, m, D_MODEL), lambda b: (b, 0, 0)), _full((1, D_MODEL)),
                  _full((D_MODEL, D_MODEL)), _full((D_MODEL, D_MODEL))],
        out_specs=[kv_spec, kv_spec, hb_spec, hb_spec],
        out_shape=[kv_shape, kv_shape, hb_shape, hb_shape],
        compiler_params=_params(1),
        name="memkv",
    )(mem, w["norm_memkv"], w["w_mem_k"], w["w_mem_v"])


def _softmax_rows(s):
    e = jnp.exp(s - jnp.max(s, axis=-1, keepdims=True))
    return e / jnp.sum(e, axis=-1, keepdims=True)


def _attn_prompt_body(q_ref, kb_ref, vb_ref, x_ref, wo_ref, out_ref, o_scr):
    scale = MEM_HEAD_DIM ** -0.5
    for h in range(MEM_HEADS):
        lo = h * MEM_HEAD_DIM
        s = lax.dot_general(q_ref[:, lo:lo + MEM_HEAD_DIM], kb_ref[h], NT_DIMS, preferred_element_type=F32) * scale
        o_scr[:, lo:lo + MEM_HEAD_DIM] = jnp.dot(_softmax_rows(s).astype(BF16), vb_ref[h], preferred_element_type=F32)
    out_ref[...] = x_ref[...] + _bdot(o_scr[...], wo_ref[...])


def _attn_prompt(q, kb, vb, x1, w, bn, seq, lq):
    m = kb.shape[2]
    r3 = lambda a: a.reshape(bn, seq, D_MODEL)
    blk = pl.BlockSpec((None, lq, D_MODEL), lambda b, i: (b, i, 0))
    kv_spec = pl.BlockSpec((None, MEM_HEADS, m, MEM_HEAD_DIM), lambda b, i: (b, 0, 0, 0))
    out = pl.pallas_call(
        _attn_prompt_body,
        grid=(bn, seq // lq),
        in_specs=[blk, kv_spec, kv_spec, blk, _full((D_MODEL, D_MODEL))],
        out_specs=blk,
        out_shape=jax.ShapeDtypeStruct((bn, seq, D_MODEL), F32),
        scratch_shapes=[pltpu.VMEM((lq, D_MODEL), F32)],
        compiler_params=_params(2),
        name="attn_prompt",
    )(r3(q), kb, vb, r3(x1), w["w_mem_o"])
    return out.reshape(bn * seq, D_MODEL)


def _attn_sample_body(q_ref, k_ref, v_ref, x_ref, wo_ref, out_ref, o_scr):
    rows = q_ref.shape[0]
    m = k_ref.shape[0]
    k2 = k_ref[...].reshape(m * MEM_HEADS, MEM_HEAD_DIM).astype(BF16)
    v2 = v_ref[...].reshape(m * MEM_HEADS, MEM_HEAD_DIM).astype(BF16)
    q = q_ref[...].astype(F32)
    q4 = jnp.concatenate([q[:, h * MEM_HEAD_DIM:(h + 1) * MEM_HEAD_DIM] for h in range(MEM_HEADS)], axis=0)
    s = lax.dot_general(q4.astype(BF16), k2, NT_DIMS, preferred_element_type=F32) * (MEM_HEAD_DIM ** -0.5)
    shape = (MEM_HEADS * rows, m * MEM_HEADS)
    q_head = lax.broadcasted_iota(jnp.int32, shape, 0) // rows
    k_head = lax.broadcasted_iota(jnp.int32, shape, 1) % MEM_HEADS
    p = _softmax_rows(jnp.where(q_head == k_head, s, -jnp.inf))
    o4 = jnp.dot(p.astype(BF16), v2, preferred_element_type=F32)
    for h in range(MEM_HEADS):
        o_scr[:, h * MEM_HEAD_DIM:(h + 1) * MEM_HEAD_DIM] = o4[h * rows:(h + 1) * rows, :]
    out_ref[...] = x_ref[...] + _bdot(o_scr[...], wo_ref[...])


def _attn_sample(q, k5, v5, layer, x1, w, bn, rows):
    m = k5.shape[2]
    r3 = lambda a: a.reshape(bn, rows, D_MODEL)
    blk = pl.BlockSpec((None, rows, D_MODEL), lambda b: (b, 0, 0))
    kv_spec = pl.BlockSpec((None, None, m, MEM_HEADS, MEM_HEAD_DIM), lambda b: (layer, b, 0, 0, 0))
    out = pl.pallas_call(
        _attn_sample_body,
        grid=(bn,),
        in_specs=[blk, kv_spec, kv_spec, blk, _full((D_MODEL, D_MODEL))],
        out_specs=blk,
        out_shape=jax.ShapeDtypeStruct((bn, rows, D_MODEL), F32),
        scratch_shapes=[pltpu.VMEM((rows, D_MODEL), F32)],
        compiler_params=_params(1),
        name="attn_sample",
    )(r3(q), k5, v5, r3(x1), w["w_mem_o"])
    return out.reshape(bn * rows, D_MODEL)


def _first_max(vals):
    m = functools.reduce(jnp.maximum, vals)
    idx = jnp.full(m.shape, len(vals) - 1, jnp.int32)
    for j in range(len(vals) - 2, -1, -1):
        idx = jnp.where(vals[j] == m, j, idx)
    return m, idx


def _route_body(x_ref, g_ref, wr_ref, br_ref, xn_ref, eid_ref, rank_ref, wt_ref, cnt_ref, carry_scr, *, tm):
    @pl.when(pl.program_id(0) == 0)
    def _():
        carry_scr[...] = jnp.zeros(carry_scr.shape, F32)

    xn = _rms(x_ref[...], g_ref[...])
    xn_ref[...] = xn
    lg = lax.dot_general(wr_ref[...], xn, NT_DIMS, precision=HIGHEST, preferred_element_type=F32) + br_ref[...]
    grp = [lg[j:j + 1, :] for j in range(MOE_GROUPS)]
    gmax, gid = _first_max(grp)
    gw = 1.0 / functools.reduce(jnp.add, [jnp.exp(r - gmax) for r in grp])
    ex = [lg[SUBLANES + j:SUBLANES + j + 1, :] for j in range(MOE_EXPERTS)]
    sel = []
    for j in range(MOE_EXPERTS_PER_GROUP):
        pick = ex[(MOE_GROUPS - 1) * MOE_EXPERTS_PER_GROUP + j]
        for g in range(MOE_GROUPS - 2, -1, -1):
            pick = jnp.where(gid == g, ex[g * MOE_EXPERTS_PER_GROUP + j], pick)
        sel.append(pick)
    v1, i1 = _first_max(sel)
    rest = [jnp.where(i1 == j, -jnp.inf, sel[j]) for j in range(MOE_EXPERTS_PER_GROUP)]
    v2, i2 = _first_max(rest)
    e2 = jnp.exp(v2 - v1)
    den = 1.0 + e2
    eid1 = gid * MOE_EXPERTS_PER_GROUP + i1
    eid2 = gid * MOE_EXPERTS_PER_GROUP + i2

    e_iota = lax.broadcasted_iota(jnp.int32, (MOE_EXPERTS, tm), 0)
    m1 = e_iota == eid1
    m2 = e_iota == eid2
    onehot = jnp.where(m1, 1.0, 0.0) + jnp.where(m2, 1.0, 0.0)
    before = lax.broadcasted_iota(jnp.int32, (tm, tm), 0) < lax.broadcasted_iota(jnp.int32, (tm, tm), 1)
    ranks = _bdot(onehot, jnp.where(before, 1.0, 0.0)) + carry_scr[:, 0:1]
    r1 = jnp.sum(jnp.where(m1, ranks, 0.0), axis=0, keepdims=True)
    r2 = jnp.sum(jnp.where(m2, ranks, 0.0), axis=0, keepdims=True)
    carry_scr[...] = carry_scr[...] + jnp.sum(onehot, axis=1, keepdims=True)
    cnt_ref[...] = carry_scr[...]

    zeros_i = jnp.zeros((SUBLANES - MOE_TOP_K, tm), jnp.int32)
    eid_ref[0:1, :] = eid1
    eid_ref[1:2, :] = eid2
    eid_ref[MOE_TOP_K:, :] = zeros_i
    rank_ref[0:1, :] = r1.astype(jnp.int32)
    rank_ref[1:2, :] = r2.astype(jnp.int32)
    rank_ref[MOE_TOP_K:, :] = zeros_i
    wt_ref[0:1, :] = gw / den
    wt_ref[1:2, :] = gw * e2 / den
    wt_ref[MOE_TOP_K:, :] = jnp.zeros((SUBLANES - MOE_TOP_K, tm), F32)


def _route(x, w, tm):
    t = x.shape[0]
    lane_blk = pl.BlockSpec((SUBLANES, tm), lambda i: (0, i))
    nr = w["w_router_t"].shape[0]
    return pl.pallas_call(
        functools.partial(_route_body, tm=tm),
        grid=(t // tm,),
        in_specs=[pl.BlockSpec((tm, D_MODEL), lambda i: (i, 0)), _full((1, D_MODEL)), _full((nr, D_MODEL)), _full((nr, 1))],
        out_specs=[pl.BlockSpec((tm, D_MODEL), lambda i: (i, 0)), lane_blk, lane_blk, lane_blk,
                   _full((MOE_EXPERTS, LANES))],
        out_shape=[jax.ShapeDtypeStruct((t, D_MODEL), F32), jax.ShapeDtypeStruct((SUBLANES, t), jnp.int32),
                   jax.ShapeDtypeStruct((SUBLANES, t), jnp.int32), jax.ShapeDtypeStruct((SUBLANES, t), F32),
                   jax.ShapeDtypeStruct((MOE_EXPERTS, LANES), F32)],
        scratch_shapes=[pltpu.VMEM((MOE_EXPERTS, LANES), F32)],
        compiler_params=_params(1),
        name="route",
    )(x, w["norm_ffn"], w["w_router_t"], w["b_router_t"])


def _row_copy(src, src_row, dst, dst_row, sem):
    return pltpu.make_async_copy(src.at[pl.ds(src_row, 1), :], dst.at[pl.ds(dst_row, 1), :], sem)


META_ENDS = 0
META_SIZES = MOE_EXPERTS
META_USED = 2 * MOE_EXPERTS


def _dispatch_body(meta_ref, dest_ref, xn_ref, out_ref, zbuf, sem, zsem, *, tm, rows, n_blocks):
    @pl.when(pl.program_id(0) == 0)
    def _():
        zbuf[...] = jnp.zeros(zbuf.shape, F32)

        def zero_fills(act):
            for e in range(MOE_EXPERTS):
                @pl.when(meta_ref[META_SIZES + e] > 0)
                def _(e=e):
                    lo = pl.multiple_of(meta_ref[META_ENDS + e] - rows, rows)
                    act(pltpu.make_async_copy(zbuf, out_ref.at[pl.ds(lo, rows), :], zsem))
            for j in range(MOE_EXPERTS):
                @pl.when(meta_ref[META_USED] + j < n_blocks)
                def _(j=j):
                    lo = pl.multiple_of((meta_ref[META_USED] + j) * rows, rows)
                    act(pltpu.make_async_copy(zbuf, out_ref.at[pl.ds(lo, rows), :], zsem))

        zero_fills(lambda cp: cp.start())
        zero_fills(lambda cp: cp.wait())

    def issue(t, carry):
        for k in range(MOE_TOP_K):
            _row_copy(xn_ref, t, out_ref, dest_ref[k, t], sem).start()
        return carry

    def drain(t, carry):
        for k in range(MOE_TOP_K):
            _row_copy(xn_ref, t, out_ref, dest_ref[k, t], sem).wait()
        return carry

    lax.fori_loop(0, tm, issue, 0, unroll=DMA_UNROLL)
    lax.fori_loop(0, tm, drain, 0, unroll=DMA_UNROLL)


def _dispatch(xn, dest, meta, n_blocks, rows, tm):
    t = xn.shape[0]
    grid_spec = pltpu.PrefetchScalarGridSpec(
        num_scalar_prefetch=1,
        grid=(t // tm,),
        in_specs=[pl.BlockSpec((MOE_TOP_K, tm), lambda i, meta: (0, i), memory_space=pltpu.SMEM),
                  pl.BlockSpec((tm, D_MODEL), lambda i, meta: (i, 0))],
        out_specs=pl.BlockSpec(memory_space=pl.ANY),
        scratch_shapes=[pltpu.VMEM((rows, D_MODEL), F32), pltpu.SemaphoreType.DMA(()), pltpu.SemaphoreType.DMA(())],
    )
    return pl.pallas_call(
        functools.partial(_dispatch_body, tm=tm, rows=rows, n_blocks=n_blocks),
        grid_spec=grid_spec,
        out_shape=jax.ShapeDtypeStruct((n_blocks * rows, D_MODEL), F32),
        compiler_params=_params(1, disable_bounds_checks=True),
        name="dispatch",
    )(meta, dest, xn)


def _ffn_body(be_ref, meta_ref, x_ref, wg_ref, wu_ref, wd_ref, o_ref, wg_scr, wu_scr, wd_scr):
    i = pl.program_id(0)
    last = jnp.maximum(meta_ref[META_USED] - 1, 0)
    used = i < meta_ref[META_USED]
    e_now = be_ref[jnp.minimum(i, last)]
    e_before = be_ref[jnp.minimum(jnp.maximum(i - 1, 0), last)]

    @pl.when(jnp.logical_or(i == 0, e_now != e_before))
    def _():
        wg_scr[...] = wg_ref[...].astype(BF16)
        wu_scr[...] = wu_ref[...].astype(BF16)
        wd_scr[...] = wd_ref[...].astype(BF16)

    @pl.when(used)
    def _():
        x = x_ref[...].astype(BF16)
        hid = (_silu(jnp.dot(x, wg_scr[...], preferred_element_type=F32))
               * jnp.dot(x, wu_scr[...], preferred_element_type=F32))
        o_ref[...] = jnp.dot(hid.astype(BF16), wd_scr[...], preferred_element_type=F32)

    @pl.when(jnp.logical_not(used))
    def _():
        o_ref[...] = jnp.zeros(o_ref.shape, F32)


def _ffn(xs, block_e, meta, w, layer, rows):
    n_slots = xs.shape[0]
    last_used = lambda i, meta: jnp.minimum(i, jnp.maximum(meta[META_USED] - 1, 0))
    blk_i = lambda i, be, meta: (last_used(i, meta), 0)
    exp_i = lambda i, be, meta: (layer, be[last_used(i, meta)], 0, 0)
    grid_spec = pltpu.PrefetchScalarGridSpec(
        num_scalar_prefetch=2,
        grid=(n_slots // rows,),
        in_specs=[pl.BlockSpec((rows, D_MODEL), blk_i),
                  pl.BlockSpec((None, None, D_MODEL, MOE_D_FF), exp_i),
                  pl.BlockSpec((None, None, D_MODEL, MOE_D_FF), exp_i),
                  pl.BlockSpec((None, None, MOE_D_FF, D_MODEL), exp_i)],
        out_specs=pl.BlockSpec((rows, D_MODEL), lambda i, be, meta: (i, 0)),
        scratch_shapes=[pltpu.VMEM((D_MODEL, MOE_D_FF), BF16), pltpu.VMEM((D_MODEL, MOE_D_FF), BF16),
                        pltpu.VMEM((MOE_D_FF, D_MODEL), BF16)],
    )
    return pl.pallas_call(
        _ffn_body,
        grid_spec=grid_spec,
        out_shape=jax.ShapeDtypeStruct((n_slots, D_MODEL), F32),
        compiler_params=_params(1),
        name="ffn",
    )(block_e, meta, xs, w["w_expert_gate"], w["w_expert_up"], w["w_expert_down"])


def _combine_body(dest_ref, wt_ref, x_ref, ys_ref, gfin_ref, out_ref, buf0, buf1, sem, *, tm, final):
    bufs = (buf0, buf1)

    def issue(t, carry):
        for k in range(MOE_TOP_K):
            _row_copy(ys_ref, dest_ref[k, t], bufs[k], t, sem).start()
        return carry

    def drain(t, carry):
        for k in range(MOE_TOP_K):
            _row_copy(ys_ref, dest_ref[k, t], bufs[k], t, sem).wait()
        return carry

    lax.fori_loop(0, tm, issue, 0, unroll=DMA_UNROLL)
    lax.fori_loop(0, tm, drain, 0, unroll=DMA_UNROLL)
    wt = wt_ref[...].T
    out = x_ref[...] + (wt[:, 0:1] * buf0[...] + wt[:, 1:2] * buf1[...])
    if final:
        out = _rms(out, gfin_ref[...])
    out_ref[...] = out


def _combine(x, ys, dest, wt, g_final, tm, final):
    t = x.shape[0]
    return pl.pallas_call(
        functools.partial(_combine_body, tm=tm, final=final),
        grid=(t // tm,),
        in_specs=[pl.BlockSpec((MOE_TOP_K, tm), lambda i: (0, i), memory_space=pltpu.SMEM),
                  pl.BlockSpec((SUBLANES, tm), lambda i: (0, i)),
                  pl.BlockSpec((tm, D_MODEL), lambda i: (i, 0)),
                  pl.BlockSpec(memory_space=pl.ANY),
                  _full((1, D_MODEL))],
        out_specs=pl.BlockSpec((tm, D_MODEL), lambda i: (i, 0)),
        out_shape=jax.ShapeDtypeStruct((t, D_MODEL), F32),
        scratch_shapes=[pltpu.VMEM((tm, D_MODEL), F32), pltpu.VMEM((tm, D_MODEL), F32), pltpu.SemaphoreType.DMA(())],
        compiler_params=_params(1, disable_bounds_checks=True),
        name="combine",
    )(dest, wt, x, ys, g_final)


def _moe(x, w, layer, g_final, final, tm, rows):
    t = x.shape[0]
    xn, eid, rank, wt, cnt = _route(x, w, tm)
    counts = cnt[:, 0].astype(jnp.int32)
    padded = (counts + rows - 1) // rows * rows
    pad_ends = jnp.cumsum(padded)
    pad_starts = pad_ends - padded
    n_blocks = (t * MOE_TOP_K) // rows + MOE_EXPERTS
    e_ids = jnp.arange(MOE_EXPERTS, dtype=jnp.int32)
    start_of = jnp.sum(jnp.where(eid[:MOE_TOP_K, :, None] == e_ids, pad_starts, 0), axis=-1)
    dest = start_of + rank[:MOE_TOP_K]
    block_lo = jnp.arange(n_blocks, dtype=jnp.int32) * rows
    block_e = jnp.minimum(jnp.sum((pad_ends[None, :] <= block_lo[:, None]).astype(jnp.int32), axis=1), MOE_EXPERTS - 1)
    meta = jnp.concatenate([pad_ends, padded, pad_ends[-1:] // rows]).astype(jnp.int32)
    xs = _dispatch(xn, dest, meta, n_blocks, rows, tm)
    ys = _ffn(xs, block_e, meta, w, layer, rows)
    return _combine(x, ys, dest, wt, g_final, tm, final)


def _expansion_tables():
    k = jnp.arange(SPLIT_PARTS * LANES, dtype=jnp.int32)[:, None] % LANES
    ch = jnp.arange(SSM_D_INNER, dtype=jnp.int32)[None, :] // SSM_HEAD_DIM
    col = jnp.arange(SSM_HEADS * LANES, dtype=jnp.int32)[None, :] // LANES
    return (k == ch).astype(BF16), (k == col).astype(BF16)


def _layer_weights(l, p):
    row = lambda a: a[l].reshape(1, -1)
    s1 = SSM_D_INNER
    s2 = s1 + SSM_CONV_DIM
    s3 = s2 + SSM_HEADS
    w_in = p["w_in"][l]
    pad_h = LANES - SSM_HEADS
    wr = jnp.zeros((SUBLANES + MOE_EXPERTS, D_MODEL), F32)
    wr = wr.at[:MOE_GROUPS].set(p["w_router_group"][l].T).at[SUBLANES:].set(p["w_router_expert"][l].T)
    br = jnp.zeros((SUBLANES + MOE_EXPERTS, 1), F32)
    br = br.at[:MOE_GROUPS, 0].set(p["b_router_group"][l]).at[SUBLANES:, 0].set(p["b_router_expert"][l].reshape(-1))
    expand_ch, expand_col = _expansion_tables()
    return {
        "norm_mix": row(p["norm_mix"]),
        "w_z": w_in[:, :s1].astype(BF16),
        "w_xbc": w_in[:, s1:s2].astype(BF16),
        "w_dt": jnp.pad(w_in[:, s2:s3], ((0, 0), (0, pad_h))).astype(BF16),
        "w_u": w_in[:, s3:s3 + GM_WIDTH].astype(BF16),
        "w_v": w_in[:, s3 + GM_WIDTH:].astype(BF16),
        "gm_ln_g": row(p["gm_ln_g"]), "gm_ln_b": row(p["gm_ln_b"]),
        "conv_w": p["conv_w"][l], "conv_b": row(p["conv_b"]),
        "dt_bias": jnp.pad(row(p["dt_bias"]), ((0, 0), (0, pad_h))),
        "a_log": jnp.pad(row(p["a_log"]), ((0, 0), (0, pad_h))),
        "d_skip": jnp.repeat(p["d_skip"][l], SSM_HEAD_DIM).reshape(1, -1),
        "ssm_norm": row(p["ssm_norm"]),
        "gm_ws": p["gm_ws"][l], "gm_bs_t": p["gm_bs"][l].T,
        "expand_ch": expand_ch, "expand_col": expand_col,
        "w_gate": p["w_gate"][l].astype(BF16), "b_gate": row(p["b_gate"]),
        "w_a_proj": p["w_a_proj"][l].astype(BF16), "w_b_proj": p["w_b_proj"][l].astype(BF16),
        "w_out": p["w_out"][l].astype(BF16),
        "norm_mem": row(p["norm_mem"]), "norm_memkv": row(p["norm_memkv"]),
        "w_mem_q": p["w_mem_q"][l].astype(BF16), "w_mem_k": p["w_mem_k"][l].astype(BF16),
        "w_mem_v": p["w_mem_v"][l].astype(BF16), "w_mem_o": p["w_mem_o"][l].astype(BF16),
        "norm_ffn": row(p["norm_ffn"]),
        "w_router_t": wr, "b_router_t": br,
        "w_expert_gate": p["w_expert_gate"], "w_expert_up": p["w_expert_up"], "w_expert_down": p["w_expert_down"],
    }


def kernel(x_prompt, x_sample, mem_prompt, state_ssm, state_conv, cache_mem_k, cache_mem_v, norm_mix, w_in, conv_w, conv_b, dt_bias, a_log, d_skip, ssm_norm, w_a_proj, gm_ln_g, gm_ln_b, gm_ws, gm_bs, w_b_proj, w_gate, b_gate, w_out, norm_mem, norm_memkv, w_mem_q, w_mem_k, w_mem_v, w_mem_o, norm_ffn, w_router_group, b_router_group, w_router_expert, b_router_expert, w_expert_gate, w_expert_up, w_expert_down, norm_final):
    p = dict(norm_mix=norm_mix, w_in=w_in, conv_w=conv_w, conv_b=conv_b, dt_bias=dt_bias, a_log=a_log, d_skip=d_skip,
             ssm_norm=ssm_norm, w_a_proj=w_a_proj, gm_ln_g=gm_ln_g, gm_ln_b=gm_ln_b, gm_ws=gm_ws, gm_bs=gm_bs,
             w_b_proj=w_b_proj, w_gate=w_gate, b_gate=b_gate, w_out=w_out, norm_mem=norm_mem, norm_memkv=norm_memkv,
             w_mem_q=w_mem_q, w_mem_k=w_mem_k, w_mem_v=w_mem_v, w_mem_o=w_mem_o, norm_ffn=norm_ffn,
             w_router_group=w_router_group, b_router_group=b_router_group, w_router_expert=w_router_expert,
             b_router_expert=b_router_expert, w_expert_gate=w_expert_gate, w_expert_up=w_expert_up,
             w_expert_down=w_expert_down)
    depth = w_in.shape[0]
    bp, lp, _ = x_prompt.shape
    bs, ls, _ = x_sample.shape
    assert lp % PROMPT_CHUNK == 0 and ls <= SAMPLE_ROWS
    g_final = norm_final.reshape(1, -1)
    hp = SSM_HEADS * SSM_HEAD_DIM
    state_shape = (SSM_HEADS, SSM_HEAD_DIM, SSM_STATE)
    state_ssm2 = state_ssm.reshape(depth, bs, hp, SSM_STATE)

    xp = x_prompt.reshape(bp * lp, D_MODEL)
    xs = jnp.pad(x_sample, ((0, 0), (0, SAMPLE_ROWS - ls), (0, 0))).reshape(bs * SAMPLE_ROWS, D_MODEL)
    tp, ts = bp * lp, bs * SAMPLE_ROWS
    tm_p, tm_s = min(256, tp), min(256, ts)
    tmm_p, tmm_s = min(512, tp), min(512, ts)
    lq_p = min(1024, lp)

    ssm_p, conv_p, mk_p, mv_p, ssm_s, conv_s, gv_s = [], [], [], [], [], [], []
    for l in range(depth):
        w = _layer_weights(l, p)
        final = l == depth - 1
        k_p, v_p, kb, vb = _memkv(mem_prompt, w)
        z, xbc, dt, u, v = _inproj(xp, w, tm_p)
        y, sg, h_new, c_new = _mixer_prompt(z, xbc, dt, u, v, w, bp, lp)
        x1, q = _merge(xp, y, sg, w, tm_p)
        x2 = _attn_prompt(q, kb, vb, x1, w, bp, lp, lq_p)
        xp = _moe(x2, w, l, g_final, final, tmm_p, MOE_ROWS_PROMPT)
        ssm_p.append(h_new.reshape((bp,) + state_shape))
        conv_p.append(c_new)
        mk_p.append(k_p)
        mv_p.append(v_p)
        z, xbc, dt, u, v = _inproj(xs, w, tm_s)
        stack_two = depth == 2 and l == 1
        y, sg, h_new, c_new = _mixer_sample(z, xbc, dt, u, v, w, bs, ls, state_ssm2, state_conv, l,
                                            ssm_s[0] if stack_two else None)
        x1, q = _merge(xs, y, sg, w, tm_s)
        x2 = _attn_sample(q, cache_mem_k, cache_mem_v, l, x1, w, bs, SAMPLE_ROWS)
        xs = _moe(x2, w, l, g_final, final, tmm_s, MOE_ROWS_SAMPLE)
        if stack_two:
            ssm_s = h_new.reshape((depth, bs) + state_shape)
        else:
            ssm_s.append(h_new)
        conv_s.append(c_new)
        gv_s.append(v.reshape(bs, SAMPLE_ROWS, GM_WIDTH)[:, :ls])
    if isinstance(ssm_s, list):
        ssm_s = jnp.stack(ssm_s).reshape((depth, bs) + state_shape)
    y_prompt = xp.reshape(bp, lp, D_MODEL)
    y_sample = xs.reshape(bs, SAMPLE_ROWS, D_MODEL)[:, :ls]
    return (y_prompt, y_sample, jnp.stack(ssm_p), jnp.stack(conv_p), jnp.stack(mk_p), jnp.stack(mv_p),
            ssm_s, jnp.stack(conv_s), jnp.stack(gv_s))
```

```python
import functools

import jax
import jax.numpy as jnp
from jax import lax
from jax.experimental import pallas as pl
from jax.experimental.pallas import tpu as pltpu

F32 = jnp.float32
BF16 = jnp.bfloat16
HIGHEST = lax.Precision.HIGHEST

NORM_EPS = 1e-6
D_MODEL = 1024
SSM_D_INNER = 1536
SSM_HEAD_DIM = 64
SSM_HEADS = 24
SSM_GROUPS = 4
SSM_HEADS_PER_GROUP = 6
SSM_STATE = 128
SSM_CONV = 4
SSM_CONV_DIM = 2560
SSM_GROUP_WIDTH = SSM_D_INNER // SSM_GROUPS
GM_WIDTH = 512
GM_GROUPS = 4
GM_GROUP_DIM = 128
MEM_HEADS = 4
MEM_HEAD_DIM = 256
MOE_GROUPS = 4
MOE_EXPERTS_PER_GROUP = 4
MOE_EXPERTS = 16
MOE_TOP_K = 2
MOE_D_FF = 512

LANES = 128
SUBLANES = 8
SAMPLE_ROWS = 8
SAMPLE_SEQS_PER_STEP = 4
ATTN_SEQS_PER_STEP = 2
PROMPT_CHUNK = 128
PROMPT_ROWS = 256
MOE_ROWS_PROMPT = 512
MOE_ROWS_SAMPLE = 128
SPLIT_PARTS = 3
DMA_UNROLL = 8
VMEM_LIMIT = 48 * 1024 * 1024
PROMPT_BLOCK_VMEM_LIMIT = 56 * 1024 * 1024

NT_DIMS = (((1,), (1,)), ((), ()))
TN_DIMS = (((0,), (0,)), ((), ()))


def _params(n_axes, **kw):
    return pltpu.CompilerParams(dimension_semantics=("arbitrary",) * n_axes, vmem_limit_bytes=VMEM_LIMIT, **kw)


def _rms(x, g):
    return x * lax.rsqrt(jnp.mean(x * x, axis=-1, keepdims=True) + NORM_EPS) * g


def _silu(x):
    return x * jax.nn.sigmoid(x)


def _gelu(x):
    return 0.5 * x * (1.0 + lax.erf(x * (2.0 ** -0.5)))


def _bdot(a, b):
    return jnp.dot(a.astype(BF16), b.astype(BF16), preferred_element_type=F32)


def _full(shape):
    n = len(shape)
    return pl.BlockSpec(shape, lambda *_: (0,) * n)


def _gm_uv(xn, wu_ref, wv_ref, lng_ref, lnb_ref):
    u = _gelu(jnp.dot(xn, wu_ref[...], preferred_element_type=F32))
    v = _gelu(jnp.dot(xn, wv_ref[...], preferred_element_type=F32))
    vc = v - jnp.mean(v, axis=-1, keepdims=True)
    var = jnp.mean(vc * vc, axis=-1, keepdims=True)
    return u, vc * lax.rsqrt(var + NORM_EPS) * lng_ref[...] + lnb_ref[...]


def _inproj_body(x_ref, g_ref, wz_ref, wx_ref, wdt_ref, wu_ref, wv_ref, lng_ref, lnb_ref,
                 z_ref, xbc_ref, dt_ref, u_ref, v_ref):
    xn = _rms(x_ref[...], g_ref[...]).astype(BF16)
    z_ref[...] = jnp.dot(xn, wz_ref[...], preferred_element_type=F32)
    xbc_ref[...] = jnp.dot(xn, wx_ref[...], preferred_element_type=F32)
    dt_ref[...] = jnp.dot(xn, wdt_ref[...], preferred_element_type=F32)
    u_ref[...], v_ref[...] = _gm_uv(xn, wu_ref, wv_ref, lng_ref, lnb_ref)


def _inproj(x, w, tm):
    t = x.shape[0]
    row = lambda n: pl.BlockSpec((tm, n), lambda i: (i, 0))
    outs = [SSM_D_INNER, SSM_CONV_DIM, LANES, GM_WIDTH, GM_WIDTH]
    return pl.pallas_call(
        _inproj_body,
        grid=(t // tm,),
        in_specs=[row(D_MODEL), _full((1, D_MODEL)), _full((D_MODEL, SSM_D_INNER)), _full((D_MODEL, SSM_CONV_DIM)),
                  _full((D_MODEL, LANES)), _full((D_MODEL, GM_WIDTH)), _full((D_MODEL, GM_WIDTH)),
                  _full((1, GM_WIDTH)), _full((1, GM_WIDTH))],
        out_specs=[row(n) for n in outs],
        out_shape=[jax.ShapeDtypeStruct((t, n), F32) for n in outs],
        compiler_params=_params(1),
        name="inproj",
    )(x, w["norm_mix"], w["w_z"], w["w_xbc"], w["w_dt"], w["w_u"], w["w_v"], w["gm_ln_g"], w["gm_ln_b"])


CONV_HIST = SSM_CONV - 1
CONV_BASE = SUBLANES


def _split_bf16(x):
    parts = []
    rest = x
    for _ in range(SPLIT_PARTS):
        piece = rest.astype(BF16)
        parts.append(piece)
        rest = rest - piece.astype(F32)
    return jnp.concatenate(parts, axis=1)


def _bf16_round(x):
    return x.astype(BF16).astype(F32)


def _conv_silu(win_ref, lc, cw_ref, cb_ref, base=CONV_BASE):
    xc = cb_ref[...]
    for k in range(SSM_CONV):
        lo = base - CONV_HIST + k
        xc = xc + _bf16_round(cw_ref[k:k + 1, :]) * win_ref[lo:lo + lc, :]
    return _silu(xc)


def _decay_cumsum(dt_raw, dtb_ref, alog_ref, tril, lv):
    lc = dt_raw.shape[0]
    dt = jax.nn.softplus(dt_raw + dtb_ref[...])
    if lv < lc:
        dt = jnp.where(lax.broadcasted_iota(jnp.int32, (lc, LANES), 0) < lv, dt, 0.0)
    a = -jnp.exp(alog_ref[...])
    acs = jnp.dot(tril.astype(F32), dt * a, precision=HIGHEST, preferred_element_type=F32)
    return dt, acs


def _gate_norm(y, z, nw_ref, out_ref):
    yz = y * _silu(z)
    for g in range(SSM_GROUPS):
        s0 = g * SSM_GROUP_WIDTH
        part = yz[:, s0:s0 + SSM_GROUP_WIDTH]
        ms = jnp.mean(part * part, axis=-1, keepdims=True)
        out_ref[:, s0:s0 + SSM_GROUP_WIDTH] = part * lax.rsqrt(ms + NORM_EPS) * nw_ref[:, s0:s0 + SSM_GROUP_WIDTH]


def _spatial_gate(u, v, tril, ws_ref, bst_ref, out_ref):
    lc = u.shape[0]
    for g in range(GM_GROUPS):
        k0 = g * GM_GROUP_DIM
        wc = jnp.where(tril, ws_ref[g, 0:lc, 0:lc], 0.0)
        s = _bdot(wc, v[:, k0:k0 + GM_GROUP_DIM]) + bst_ref[0:lc, g:g + 1]
        out_ref[:, k0:k0 + GM_GROUP_DIM] = u[:, k0:k0 + GM_GROUP_DIM] * s


def _tril(lc):
    return lax.broadcasted_iota(jnp.int32, (lc, lc), 0) >= lax.broadcasted_iota(jnp.int32, (lc, lc), 1)


def _ssd_wide(xc, dt, acs, tril, dsk_ref, ex_ref, ecol_ref, h_scr, y_scr, acst_scr):
    lc = xc.shape[0]
    acs_parts = _split_bf16(acs)
    wide = jnp.dot(jnp.concatenate([_split_bf16(dt), acs_parts], axis=0), ex_ref[...], preferred_element_type=F32)
    dtx = wide[:lc]
    acsx = wide[lc:]
    colb = jnp.dot(acs_parts, ecol_ref[...], preferred_element_type=F32)
    acst_scr[...] = acs.T
    c_col = jnp.exp(acst_scr[:, lc - 1:lc])
    xs = xc[:, :SSM_D_INNER]
    xdt = xs * dtx
    xdt_b = xdt.astype(BF16)
    xdte_b = (xdt * jnp.exp(acsx[lc - 1:lc, :] - acsx)).astype(BF16)
    e_acsx = jnp.exp(acsx)
    skip = dsk_ref[...] * xs
    first_head = lax.broadcasted_iota(jnp.int32, (lc, LANES), 1) < SSM_HEAD_DIM
    pair = 2 * SSM_HEAD_DIM
    for g in range(SSM_GROUPS):
        b0 = SSM_D_INNER + g * SSM_STATE
        c0 = SSM_D_INNER + SSM_GROUPS * SSM_STATE + g * SSM_STATE
        g0 = g * SSM_GROUP_WIDTH
        bm = xc[:, b0:b0 + SSM_STATE].astype(BF16)
        cm = xc[:, c0:c0 + SSM_STATE].astype(BF16)
        cb = lax.dot_general(cm, bm, NT_DIMS, preferred_element_type=F32)
        h_grp = h_scr[g0:g0 + SSM_GROUP_WIDTH, :]
        y_off = lax.dot_general(cm, h_grp.astype(BF16), NT_DIMS, preferred_element_type=F32)
        for j in range(SSM_HEADS_PER_GROUP // 2):
            h1 = g * SSM_HEADS_PER_GROUP + 2 * j
            p0 = h1 * SSM_HEAD_DIM
            atts = []
            for h in (h1, h1 + 1):
                seg = colb[:, h * LANES:(h + 1) * LANES] - acst_scr[h:h + 1, :]
                atts.append((cb * jnp.where(tril, jnp.exp(seg), 0.0)).astype(BF16))
            blk = xdt_b[:, p0:p0 + pair]
            zero = jnp.zeros_like(blk)
            rhs = jnp.concatenate([jnp.where(first_head, blk, zero), jnp.where(first_head, zero, blk)], axis=0)
            y_diag = jnp.dot(jnp.concatenate(atts, axis=1), rhs, preferred_element_type=F32)
            y_scr[:, p0:p0 + pair] = (y_diag + y_off[:, j * pair:(j + 1) * pair] * e_acsx[:, p0:p0 + pair]
                                      + skip[:, p0:p0 + pair])
        st = lax.dot_general(xdte_b[:, g0:g0 + SSM_GROUP_WIDTH], bm, TN_DIMS, preferred_element_type=F32)
        for r in range(SSM_HEADS_PER_GROUP):
            h = g * SSM_HEADS_PER_GROUP + r
            r0 = r * SSM_HEAD_DIM
            h_scr[g0 + r0:g0 + r0 + SSM_HEAD_DIM, :] = (h_grp[r0:r0 + SSM_HEAD_DIM, :] * c_col[h:h + 1, :]
                                                        + st[r0:r0 + SSM_HEAD_DIM, :])


PROMPT_BLOCK_WEIGHTS = ["norm_mix", "w_z", "w_xbc", "w_dt", "w_u", "w_v", "gm_ln_g", "gm_ln_b",
                        "conv_w", "conv_b", "dt_bias", "a_log", "d_skip", "ssm_norm", "gm_ws", "gm_bs_t",
                        "expand_ch", "expand_col", "w_gate", "b_gate", "w_a_proj", "w_b_proj", "w_out",
                        "norm_mem", "w_mem_q", "w_mem_o"]


def _prompt_block_body(x_ref, kb_ref, vb_ref,
                       gmix_ref, wz_ref, wx_ref, wdt_ref, wu_ref, wv_ref, lng_ref, lnb_ref,
                       cw_ref, cb_ref, dtb_ref, alog_ref, dsk_ref, nw_ref, ws_ref, bst_ref, ex_ref, ecol_ref,
                       wg_ref, bg_ref, wa_ref, wb_ref, wo_ref, gmem_ref, wq_ref, wmo_ref,
                       x2_ref, hout_ref, cout_ref,
                       xp_scr, h_scr, y_scr, acst_scr, yn_scr, sg_scr, o_scr):
    lc = PROMPT_CHUNK
    rows = x_ref.shape[0]
    c = pl.program_id(1)

    @pl.when(c == 0)
    def _():
        xp_scr[CONV_BASE - CONV_HIST:CONV_BASE, :] = jnp.zeros((CONV_HIST, SSM_CONV_DIM), F32)
        h_scr[...] = jnp.zeros(h_scr.shape, F32)

    x = x_ref[...]
    xn = _rms(x, gmix_ref[...]).astype(BF16)
    xbc = jnp.dot(xn, wx_ref[...], preferred_element_type=F32)
    cout_ref[...] = xbc[rows - CONV_HIST:rows, :]
    xp_scr[CONV_BASE:CONV_BASE + rows, :] = _bf16_round(xbc)
    z = jnp.dot(xn, wz_ref[...], preferred_element_type=F32)
    dt_raw = jnp.dot(xn, wdt_ref[...], preferred_element_type=F32)
    u, v = _gm_uv(xn, wu_ref, wv_ref, lng_ref, lnb_ref)

    tril = _tril(lc)
    for sub in range(rows // lc):
        r0 = sub * lc
        xc = _conv_silu(xp_scr, lc, cw_ref, cb_ref, CONV_BASE + r0)
        dt, acs = _decay_cumsum(dt_raw[r0:r0 + lc], dtb_ref, alog_ref, tril, lc)
        _ssd_wide(xc, dt, acs, tril, dsk_ref, ex_ref, ecol_ref, h_scr, y_scr.at[pl.ds(r0, lc)], acst_scr.at[sub])
        _spatial_gate(u[r0:r0 + lc], v[r0:r0 + lc], tril, ws_ref, bst_ref, sg_scr.at[pl.ds(r0, lc)])
    xp_scr[CONV_BASE - CONV_HIST:CONV_BASE, :] = xp_scr[CONV_BASE + rows - CONV_HIST:CONV_BASE + rows, :]

    @pl.when(c == pl.num_programs(1) - 1)
    def _():
        hout_ref[...] = h_scr[...]

    _gate_norm(y_scr[...], z, nw_ref, yn_scr)
    x1 = _gated_merge(x, xn, yn_scr[...], sg_scr[...], wg_ref, bg_ref, wa_ref, wb_ref, wo_ref)

    q = _bdot(_rms(x1, gmem_ref[...]), wq_ref[...]).astype(BF16)
    scale = MEM_HEAD_DIM ** -0.5
    for h in range(MEM_HEADS):
        lo = h * MEM_HEAD_DIM
        s = lax.dot_general(q[:, lo:lo + MEM_HEAD_DIM], kb_ref[h], NT_DIMS, preferred_element_type=F32) * scale
        o_scr[:, lo:lo + MEM_HEAD_DIM] = jnp.dot(_softmax_rows(s).astype(BF16), vb_ref[h], preferred_element_type=F32)
    x2_ref[...] = x1 + _bdot(o_scr[...], wmo_ref[...])


def _mixer_weight_specs(w, names):
    return [_full(w[n].shape) for n in names], [w[n] for n in names]


def _prompt_block(x, kb, vb, w, bn, seq):
    lc = PROMPT_ROWS if seq % PROMPT_ROWS == 0 else PROMPT_CHUNK
    hp = SSM_HEADS * SSM_HEAD_DIM
    m = kb.shape[2]
    blk = pl.BlockSpec((None, lc, D_MODEL), lambda b, c: (b, c, 0))
    kv_spec = pl.BlockSpec((None, MEM_HEADS, m, MEM_HEAD_DIM), lambda b, c: (b, 0, 0, 0))
    once = lambda shape: pl.BlockSpec(shape, lambda b, c: (0,) * len(shape), pipeline_mode=pl.Buffered(1))
    x2, h_new, conv_new = pl.pallas_call(
        _prompt_block_body,
        grid=(bn, seq // lc),
        in_specs=[blk, kv_spec, kv_spec] + [once(w[n].shape) for n in PROMPT_BLOCK_WEIGHTS],
        out_specs=[blk,
                   pl.BlockSpec((None, hp, SSM_STATE), lambda b, c: (b, 0, 0)),
                   pl.BlockSpec((None, CONV_HIST, SSM_CONV_DIM), lambda b, c: (b, 0, 0))],
        out_shape=[jax.ShapeDtypeStruct((bn, seq, D_MODEL), F32),
                   jax.ShapeDtypeStruct((bn, hp, SSM_STATE), F32),
                   jax.ShapeDtypeStruct((bn, CONV_HIST, SSM_CONV_DIM), F32)],
        scratch_shapes=[pltpu.VMEM((CONV_BASE + lc, SSM_CONV_DIM), F32),
                        pltpu.VMEM((hp, SSM_STATE), F32),
                        pltpu.VMEM((lc, SSM_D_INNER), F32),
                        pltpu.VMEM((lc // PROMPT_CHUNK, LANES, PROMPT_CHUNK), F32),
                        pltpu.VMEM((lc, SSM_D_INNER), F32),
                        pltpu.VMEM((lc, GM_WIDTH), F32),
                        pltpu.VMEM((lc, D_MODEL), F32)],
        compiler_params=pltpu.CompilerParams(dimension_semantics=("arbitrary", "arbitrary"),
                                             vmem_limit_bytes=PROMPT_BLOCK_VMEM_LIMIT),
        name="prompt_block",
    )(x.reshape(bn, seq, D_MODEL), kb, vb, *[w[n] for n in PROMPT_BLOCK_WEIGHTS])
    return x2.reshape(bn * seq, D_MODEL), h_new, conv_new


def _ssd_short(xc, dt, acs, lv, dsk_ref, ex_ref, h_ref, hout_ref):
    lc = xc.shape[0]
    tot = acs[lc - 1:lc, :]
    e_acs = jnp.exp(acs)
    dt_end = dt * jnp.exp(tot - acs)
    c_col = jnp.exp(acs.T[:, lc - 1:lc])
    row = lax.broadcasted_iota(jnp.int32, (lc, LANES), 0)
    head = lax.broadcasted_iota(jnp.int32, (lc, LANES), 1)
    bms, cms, cbs = [], [], []
    for g in range(SSM_GROUPS):
        b0 = SSM_D_INNER + g * SSM_STATE
        c0 = SSM_D_INNER + SSM_GROUPS * SSM_STATE + g * SSM_STATE
        bms.append(xc[:, b0:b0 + SSM_STATE].astype(BF16))
        cms.append(xc[:, c0:c0 + SSM_STATE].astype(BF16))
        cbs.append(lax.dot_general(cms[g], bms[g], NT_DIMS, preferred_element_type=F32))
    coef = []
    for s in range(lv):
        decay = jnp.where(row >= s, jnp.exp(acs - acs[s:s + 1, :]), 0.0) * dt[s:s + 1, :]
        cb_s = jnp.zeros((lc, LANES), F32)
        for g in range(SSM_GROUPS):
            in_group = (head >= g * SSM_HEADS_PER_GROUP) & (head < (g + 1) * SSM_HEADS_PER_GROUP)
            cb_s = jnp.where(in_group, cbs[g][:, s:s + 1], cb_s)
        coef.append(decay * cb_s)
    wide = jnp.dot(_split_bf16(jnp.concatenate(coef + [e_acs, dt_end], axis=0)), ex_ref[...],
                   preferred_element_type=F32)
    xs = xc[:, :SSM_D_INNER]
    y = dsk_ref[...] * xs
    for s in range(lv):
        y = y + wide[s * lc:(s + 1) * lc] * xs[s:s + 1, :]
    e_acsx = wide[lv * lc:(lv + 1) * lc]
    xdte_b = (xs * wide[(lv + 1) * lc:]).astype(BF16)
    y_off = []
    for g in range(SSM_GROUPS):
        g0 = g * SSM_GROUP_WIDTH
        h_grp = h_ref[g0:g0 + SSM_GROUP_WIDTH, :]
        y_off.append(lax.dot_general(cms[g], h_grp.astype(BF16), NT_DIMS, preferred_element_type=F32))
        st = lax.dot_general(xdte_b[:, g0:g0 + SSM_GROUP_WIDTH], bms[g], TN_DIMS, preferred_element_type=F32)
        for r in range(SSM_HEADS_PER_GROUP):
            h = g * SSM_HEADS_PER_GROUP + r
            r0 = r * SSM_HEAD_DIM
            hout_ref[g0 + r0:g0 + r0 + SSM_HEAD_DIM, :] = (h_grp[r0:r0 + SSM_HEAD_DIM, :] * c_col[h:h + 1, :]
                                                           + st[r0:r0 + SSM_HEAD_DIM, :])
    return y + jnp.concatenate(y_off, axis=1) * e_acsx


def _mixer_sample_body(*refs, lv, sb, stack_prev):
    xbc_ref, dt_ref, z_ref, u_ref, v_ref, cprev_ref, h0_ref = refs[:7]
    i = 7
    if stack_prev:
        hprev_ref = refs[i]
        i += 1
    cw_ref, cb_ref, dtb_ref, alog_ref, dsk_ref, nw_ref, ws_ref, bst_ref, ex_ref = refs[i:i + 9]
    y_ref, sg_ref, hout_ref, cout_ref, xp_scr = refs[i + 9:]
    lc = SAMPLE_ROWS
    tril = _tril(lc)
    for s in range(sb):
        win = xp_scr.at[s]
        xbc = xbc_ref[s]
        win[CONV_BASE - CONV_HIST:CONV_BASE, :] = _bf16_round(cprev_ref[s])
        win[CONV_BASE:CONV_BASE + lc, :] = _bf16_round(xbc)
        xc = _conv_silu(win, lc, cw_ref, cb_ref)
        cout_ref[s] = xbc[lv - CONV_HIST:lv, :]
        dt, acs = _decay_cumsum(dt_ref[s], dtb_ref, alog_ref, tril, lv)
        if stack_prev:
            hout_ref[0, s] = hprev_ref[s]
            h_out = hout_ref.at[1, s]
        else:
            h_out = hout_ref.at[s]
        y = _ssd_short(xc, dt, acs, lv, dsk_ref, ex_ref, h0_ref.at[s], h_out)
        _gate_norm(y, z_ref[s], nw_ref, y_ref.at[s])
        _spatial_gate(u_ref[s], v_ref[s], tril, ws_ref, bst_ref, sg_ref.at[s])


def _mixer_sample(z, xbc, dt, u, v, w, bn, lv, state_ssm, state_conv, layer, h_prev_layer):
    lc = SAMPLE_ROWS
    sb = SAMPLE_SEQS_PER_STEP if bn % SAMPLE_SEQS_PER_STEP == 0 else 1
    stack_prev = h_prev_layer is not None
    hp = SSM_HEADS * SSM_HEAD_DIM
    r3 = lambda a: a.reshape(bn, lc, a.shape[-1])
    blk = lambda n: pl.BlockSpec((sb, lc, n), lambda b: (b, 0, 0))
    in_specs = [blk(SSM_CONV_DIM), blk(LANES), blk(SSM_D_INNER), blk(GM_WIDTH), blk(GM_WIDTH),
                pl.BlockSpec((None, sb, CONV_HIST, SSM_CONV_DIM), lambda b: (layer, b, 0, 0)),
                pl.BlockSpec((None, sb, hp, SSM_STATE), lambda b: (layer, b, 0, 0))]
    args = [r3(xbc), r3(dt), r3(z), r3(u), r3(v), state_conv, state_ssm]
    if stack_prev:
        in_specs.append(pl.BlockSpec((sb, hp, SSM_STATE), lambda b: (b, 0, 0)))
        args.append(h_prev_layer)
        h_spec = pl.BlockSpec((2, sb, hp, SSM_STATE), lambda b: (0, b, 0, 0))
        h_shape = jax.ShapeDtypeStruct((2, bn, hp, SSM_STATE), F32)
    else:
        h_spec = pl.BlockSpec((sb, hp, SSM_STATE), lambda b: (b, 0, 0))
        h_shape = jax.ShapeDtypeStruct((bn, hp, SSM_STATE), F32)
    w_specs, w_args = _mixer_weight_specs(w, ["conv_w", "conv_b", "dt_bias", "a_log", "d_skip", "ssm_norm", "gm_ws",
                                              "gm_bs_t", "expand_ch"])
    y, sg, h_new, conv_new = pl.pallas_call(
        functools.partial(_mixer_sample_body, lv=lv, sb=sb, stack_prev=stack_prev),
        grid=(bn // sb,),
        in_specs=in_specs + w_specs,
        out_specs=[blk(SSM_D_INNER), blk(GM_WIDTH), h_spec,
                   pl.BlockSpec((sb, CONV_HIST, SSM_CONV_DIM), lambda b: (b, 0, 0))],
        out_shape=[jax.ShapeDtypeStruct((bn, lc, SSM_D_INNER), F32), jax.ShapeDtypeStruct((bn, lc, GM_WIDTH), F32),
                   h_shape, jax.ShapeDtypeStruct((bn, CONV_HIST, SSM_CONV_DIM), F32)],
        scratch_shapes=[pltpu.VMEM((sb, CONV_BASE + lc, SSM_CONV_DIM), F32)],
        compiler_params=_params(1),
        name="mixer_sample",
    )(*args, *w_args)
    t = bn * lc
    return y.reshape(t, SSM_D_INNER), sg.reshape(t, GM_WIDTH), h_new, conv_new


def _gated_merge(x, xn, y, sg, wg_ref, bg_ref, wa_ref, wb_ref, wo_ref):
    gates = jax.nn.sigmoid(jnp.dot(xn, wg_ref[...], preferred_element_type=F32) + bg_ref[...])
    merged = gates[:, :D_MODEL] * _bdot(y, wa_ref[...]) + gates[:, D_MODEL:] * _bdot(sg, wb_ref[...])
    return x + _bdot(merged, wo_ref[...])


def _merge_body(x_ref, y_ref, sg_ref, gmix_ref, wg_ref, bg_ref, wa_ref, wb_ref, wo_ref, gmem_ref, wq_ref,
                x1_ref, q_ref):
    x = x_ref[...]
    xn = _rms(x, gmix_ref[...]).astype(BF16)
    x1 = _gated_merge(x, xn, y_ref[...], sg_ref[...], wg_ref, bg_ref, wa_ref, wb_ref, wo_ref)
    x1_ref[...] = x1
    q_ref[...] = _bdot(_rms(x1, gmem_ref[...]), wq_ref[...]).astype(BF16)


def _merge(x, y, sg, w, tm):
    t = x.shape[0]
    row = lambda n: pl.BlockSpec((tm, n), lambda i: (i, 0))
    return pl.pallas_call(
        _merge_body,
        grid=(t // tm,),
        in_specs=[row(D_MODEL), row(SSM_D_INNER), row(GM_WIDTH), _full((1, D_MODEL)),
                  _full((D_MODEL, 2 * D_MODEL)), _full((1, 2 * D_MODEL)), _full((SSM_D_INNER, D_MODEL)),
                  _full((GM_WIDTH, D_MODEL)), _full((D_MODEL, D_MODEL)), _full((1, D_MODEL)), _full((D_MODEL, D_MODEL))],
        out_specs=[row(D_MODEL), row(D_MODEL)],
        out_shape=[jax.ShapeDtypeStruct((t, D_MODEL), F32), jax.ShapeDtypeStruct((t, D_MODEL), BF16)],
        compiler_params=_params(1),
        name="merge",
    )(x, y, sg, w["norm_mix"], w["w_gate"], w["b_gate"], w["w_a_proj"], w["w_b_proj"], w["w_out"],
      w["norm_mem"], w["w_mem_q"])


def _memkv_body(m_ref, g_ref, wk_ref, wv_ref, k_ref, v_ref, kb_ref, vb_ref):
    mn = _rms(m_ref[...], g_ref[...]).astype(BF16)
    k = jnp.dot(mn, wk_ref[...], preferred_element_type=F32)
    v = jnp.dot(mn, wv_ref[...], preferred_element_type=F32)
    for h in range(MEM_HEADS):
        lo = h * MEM_HEAD_DIM
        k_ref[:, h, :] = k[:, lo:lo + MEM_HEAD_DIM]
        v_ref[:, h, :] = v[:, lo:lo + MEM_HEAD_DIM]
        kb_ref[h] = k[:, lo:lo + MEM_HEAD_DIM].astype(BF16)
        vb_ref[h] = v[:, lo:lo + MEM_HEAD_DIM].astype(BF16)


def _memkv(mem, w):
    bn, m, _ = mem.shape
    kv_spec = pl.BlockSpec((None, m, MEM_HEADS, MEM_HEAD_DIM), lambda b: (b, 0, 0, 0))
    kv_shape = jax.ShapeDtypeStruct((bn, m, MEM_HEADS, MEM_HEAD_DIM), F32)
    hb_spec = pl.BlockSpec((None, MEM_HEADS, m, MEM_HEAD_DIM), lambda b: (b, 0, 0, 0))
    hb_shape = jax.ShapeDtypeStruct((bn, MEM_HEADS, m, MEM_HEAD_DIM), BF16)
    return pl.pallas_call(
        _memkv_body,
        grid=(bn,),
        in_specs=[pl.BlockSpec((None, m, D_MODEL), lambda b: (b, 0, 0)), _full((1, D_MODEL)),
                  _full((D_MODEL, D_MODEL)), _full((D_MODEL, D_MODEL))],
        out_specs=[kv_spec, kv_spec, hb_spec, hb_spec],
        out_shape=[kv_shape, kv_shape, hb_shape, hb_shape],
        compiler_params=_params(1),
        name="memkv",
    )(mem, w["norm_memkv"], w["w_mem_k"], w["w_mem_v"])


def _softmax_rows(s):
    e = jnp.exp(s - jnp.max(s, axis=-1, keepdims=True))
    return e / jnp.sum(e, axis=-1, keepdims=True)


def _attn_sample_body(q_ref, k_ref, v_ref, x_ref, wo_ref, out_ref, o_scr):
    sb, rows, _ = q_ref.shape
    m = k_ref.shape[1]
    shape = (MEM_HEADS * rows, m * MEM_HEADS)
    own_head = (lax.broadcasted_iota(jnp.int32, shape, 0) // rows
                == lax.broadcasted_iota(jnp.int32, shape, 1) % MEM_HEADS)
    for i in range(sb):
        k2 = k_ref[i].reshape(m * MEM_HEADS, MEM_HEAD_DIM).astype(BF16)
        v2 = v_ref[i].reshape(m * MEM_HEADS, MEM_HEAD_DIM).astype(BF16)
        q = q_ref[i].astype(F32)
        q4 = jnp.concatenate([q[:, h * MEM_HEAD_DIM:(h + 1) * MEM_HEAD_DIM] for h in range(MEM_HEADS)], axis=0)
        s = lax.dot_general(q4.astype(BF16), k2, NT_DIMS, preferred_element_type=F32) * (MEM_HEAD_DIM ** -0.5)
        p = _softmax_rows(jnp.where(own_head, s, -jnp.inf))
        o4 = jnp.dot(p.astype(BF16), v2, preferred_element_type=F32)
        for h in range(MEM_HEADS):
            o_scr[i * rows:(i + 1) * rows, h * MEM_HEAD_DIM:(h + 1) * MEM_HEAD_DIM] = o4[h * rows:(h + 1) * rows, :]
    proj = _bdot(o_scr[...], wo_ref[...])
    for i in range(sb):
        out_ref[i] = x_ref[i] + proj[i * rows:(i + 1) * rows, :]


def _attn_sample(q, k5, v5, layer, x1, w, bn, rows):
    m = k5.shape[2]
    sb = ATTN_SEQS_PER_STEP if bn % ATTN_SEQS_PER_STEP == 0 else 1
    r3 = lambda a: a.reshape(bn, rows, D_MODEL)
    blk = pl.BlockSpec((sb, rows, D_MODEL), lambda b: (b, 0, 0))
    kv_spec = pl.BlockSpec((None, sb, m, MEM_HEADS, MEM_HEAD_DIM), lambda b: (layer, b, 0, 0, 0))
    out = pl.pallas_call(
        _attn_sample_body,
        grid=(bn // sb,),
        in_specs=[blk, kv_spec, kv_spec, blk, _full((D_MODEL, D_MODEL))],
        out_specs=blk,
        out_shape=jax.ShapeDtypeStruct((bn, rows, D_MODEL), F32),
        scratch_shapes=[pltpu.VMEM((sb * rows, D_MODEL), F32)],
        compiler_params=_params(1),
        name="attn_sample",
    )(r3(q), k5, v5, r3(x1), w["w_mem_o"])
    return out.reshape(bn * rows, D_MODEL)


def _first_max(vals):
    m = functools.reduce(jnp.maximum, vals)
    idx = jnp.full(m.shape, len(vals) - 1, jnp.int32)
    for j in range(len(vals) - 2, -1, -1):
        idx = jnp.where(vals[j] == m, j, idx)
    return m, idx


def _route_body(x_ref, g_ref, wr_ref, br_ref, xn_ref, eid_ref, rank_ref, wt_ref, cnt_ref, carry_scr, *, tm):
    @pl.when(pl.program_id(0) == 0)
    def _():
        carry_scr[...] = jnp.zeros(carry_scr.shape, F32)

    xn = _rms(x_ref[...], g_ref[...])
    xn_ref[...] = xn
    lg = lax.dot_general(wr_ref[...].astype(BF16), xn.astype(BF16), NT_DIMS, preferred_element_type=F32) + br_ref[...]
    grp = [lg[j:j + 1, :] for j in range(MOE_GROUPS)]
    gmax, gid = _first_max(grp)
    gw = 1.0 / functools.reduce(jnp.add, [jnp.exp(r - gmax) for r in grp])
    ex = [lg[SUBLANES + j:SUBLANES + j + 1, :] for j in range(MOE_EXPERTS)]
    sel = []
    for j in range(MOE_EXPERTS_PER_GROUP):
        pick = ex[(MOE_GROUPS - 1) * MOE_EXPERTS_PER_GROUP + j]
        for g in range(MOE_GROUPS - 2, -1, -1):
            pick = jnp.where(gid == g, ex[g * MOE_EXPERTS_PER_GROUP + j], pick)
        sel.append(pick)
    v1, i1 = _first_max(sel)
    rest = [jnp.where(i1 == j, -jnp.inf, sel[j]) for j in range(MOE_EXPERTS_PER_GROUP)]
    v2, i2 = _first_max(rest)
    e2 = jnp.exp(v2 - v1)
    den = 1.0 + e2
    eid1 = gid * MOE_EXPERTS_PER_GROUP + i1
    eid2 = gid * MOE_EXPERTS_PER_GROUP + i2

    e_iota = lax.broadcasted_iota(jnp.int32, (MOE_EXPERTS, tm), 0)
    m1 = e_iota == eid1
    m2 = e_iota == eid2
    onehot = jnp.where(m1, 1.0, 0.0) + jnp.where(m2, 1.0, 0.0)
    before = lax.broadcasted_iota(jnp.int32, (tm, tm), 0) < lax.broadcasted_iota(jnp.int32, (tm, tm), 1)
    ranks = _bdot(onehot, jnp.where(before, 1.0, 0.0)) + carry_scr[:, 0:1]
    r1 = jnp.sum(jnp.where(m1, ranks, 0.0), axis=0, keepdims=True)
    r2 = jnp.sum(jnp.where(m2, ranks, 0.0), axis=0, keepdims=True)
    carry_scr[...] = carry_scr[...] + jnp.sum(onehot, axis=1, keepdims=True)
    cnt_ref[...] = carry_scr[...]

    zeros_i = jnp.zeros((SUBLANES - MOE_TOP_K, tm), jnp.int32)
    eid_ref[0:1, :] = eid1
    eid_ref[1:2, :] = eid2
    eid_ref[MOE_TOP_K:, :] = zeros_i
    rank_ref[0:1, :] = r1.astype(jnp.int32)
    rank_ref[1:2, :] = r2.astype(jnp.int32)
    rank_ref[MOE_TOP_K:, :] = zeros_i
    wt_ref[0:1, :] = gw / den
    wt_ref[1:2, :] = gw * e2 / den
    wt_ref[MOE_TOP_K:, :] = jnp.zeros((SUBLANES - MOE_TOP_K, tm), F32)


def _route(x, w, tm):
    t = x.shape[0]
    lane_blk = pl.BlockSpec((SUBLANES, tm), lambda i: (0, i))
    nr = w["w_router_t"].shape[0]
    return pl.pallas_call(
        functools.partial(_route_body, tm=tm),
        grid=(t // tm,),
        in_specs=[pl.BlockSpec((tm, D_MODEL), lambda i: (i, 0)), _full((1, D_MODEL)), _full((nr, D_MODEL)), _full((nr, 1))],
        out_specs=[pl.BlockSpec((tm, D_MODEL), lambda i: (i, 0)), lane_blk, lane_blk, lane_blk,
                   _full((MOE_EXPERTS, LANES))],
        out_shape=[jax.ShapeDtypeStruct((t, D_MODEL), F32), jax.ShapeDtypeStruct((SUBLANES, t), jnp.int32),
                   jax.ShapeDtypeStruct((SUBLANES, t), jnp.int32), jax.ShapeDtypeStruct((SUBLANES, t), F32),
                   jax.ShapeDtypeStruct((MOE_EXPERTS, LANES), F32)],
        scratch_shapes=[pltpu.VMEM((MOE_EXPERTS, LANES), F32)],
        compiler_params=_params(1),
        name="route",
    )(x, w["norm_ffn"], w["w_router_t"], w["b_router_t"])


def _row_copy(src, src_row, dst, dst_row, sem):
    return pltpu.make_async_copy(src.at[pl.ds(src_row, 1), :], dst.at[pl.ds(dst_row, 1), :], sem)


META_ENDS = 0
META_SIZES = MOE_EXPERTS
META_USED = 2 * MOE_EXPERTS


def _dispatch_body(meta_ref, dest_ref, xn_ref, out_ref, zbuf, sem, zsem, *, tm, rows, n_blocks):
    @pl.when(pl.program_id(0) == 0)
    def _():
        zbuf[...] = jnp.zeros(zbuf.shape, F32)

        def zero_fills(act):
            for e in range(MOE_EXPERTS):
                @pl.when(meta_ref[META_SIZES + e] > 0)
                def _(e=e):
                    lo = pl.multiple_of(meta_ref[META_ENDS + e] - rows, rows)
                    act(pltpu.make_async_copy(zbuf, out_ref.at[pl.ds(lo, rows), :], zsem))
            for j in range(MOE_EXPERTS):
                @pl.when(meta_ref[META_USED] + j < n_blocks)
                def _(j=j):
                    lo = pl.multiple_of((meta_ref[META_USED] + j) * rows, rows)
                    act(pltpu.make_async_copy(zbuf, out_ref.at[pl.ds(lo, rows), :], zsem))

        zero_fills(lambda cp: cp.start())
        zero_fills(lambda cp: cp.wait())

    def issue(t, carry):
        for k in range(MOE_TOP_K):
            _row_copy(xn_ref, t, out_ref, dest_ref[k, t], sem).start(priority=k)
        return carry

    def drain(t, carry):
        for k in range(MOE_TOP_K):
            _row_copy(xn_ref, t, out_ref, dest_ref[k, t], sem).wait()
        return carry

    lax.fori_loop(0, tm, issue, 0, unroll=DMA_UNROLL)
    lax.fori_loop(0, tm, drain, 0, unroll=DMA_UNROLL)


def _dispatch(xn, dest, meta, n_blocks, rows, tm):
    t = xn.shape[0]
    grid_spec = pltpu.PrefetchScalarGridSpec(
        num_scalar_prefetch=1,
        grid=(t // tm,),
        in_specs=[pl.BlockSpec((MOE_TOP_K, tm), lambda i, meta: (0, i), memory_space=pltpu.SMEM),
                  pl.BlockSpec((tm, D_MODEL), lambda i, meta: (i, 0))],
        out_specs=pl.BlockSpec(memory_space=pl.ANY),
        scratch_shapes=[pltpu.VMEM((rows, D_MODEL), F32), pltpu.SemaphoreType.DMA(()), pltpu.SemaphoreType.DMA(())],
    )
    return pl.pallas_call(
        functools.partial(_dispatch_body, tm=tm, rows=rows, n_blocks=n_blocks),
        grid_spec=grid_spec,
        out_shape=jax.ShapeDtypeStruct((n_blocks * rows, D_MODEL), F32),
        compiler_params=_params(1, disable_bounds_checks=True),
        name="dispatch",
    )(meta, dest, xn)


def _ffn_body(be_ref, meta_ref, x_ref, wg_ref, wu_ref, wd_ref, o_ref, wg_scr, wu_scr, wd_scr):
    i = pl.program_id(0)
    last = jnp.maximum(meta_ref[META_USED] - 1, 0)
    used = i < meta_ref[META_USED]
    e_now = be_ref[jnp.minimum(i, last)]
    e_before = be_ref[jnp.minimum(jnp.maximum(i - 1, 0), last)]

    @pl.when(jnp.logical_or(i == 0, e_now != e_before))
    def _():
        wg_scr[...] = wg_ref[...].astype(BF16)
        wu_scr[...] = wu_ref[...].astype(BF16)
        wd_scr[...] = wd_ref[...].astype(BF16)

    @pl.when(used)
    def _():
        x = x_ref[...].astype(BF16)
        hid = (_silu(jnp.dot(x, wg_scr[...], preferred_element_type=F32))
               * jnp.dot(x, wu_scr[...], preferred_element_type=F32))
        o_ref[...] = jnp.dot(hid.astype(BF16), wd_scr[...], preferred_element_type=F32)

    @pl.when(jnp.logical_not(used))
    def _():
        o_ref[...] = jnp.zeros(o_ref.shape, F32)


def _ffn(xs, block_e, meta, w, layer, rows):
    n_slots = xs.shape[0]
    last_used = lambda i, meta: jnp.minimum(i, jnp.maximum(meta[META_USED] - 1, 0))
    blk_i = lambda i, be, meta: (last_used(i, meta), 0)
    exp_i = lambda i, be, meta: (layer, be[last_used(i, meta)], 0, 0)
    grid_spec = pltpu.PrefetchScalarGridSpec(
        num_scalar_prefetch=2,
        grid=(n_slots // rows,),
        in_specs=[pl.BlockSpec((rows, D_MODEL), blk_i),
                  pl.BlockSpec((None, None, D_MODEL, MOE_D_FF), exp_i),
                  pl.BlockSpec((None, None, D_MODEL, MOE_D_FF), exp_i),
                  pl.BlockSpec((None, None, MOE_D_FF, D_MODEL), exp_i)],
        out_specs=pl.BlockSpec((rows, D_MODEL), lambda i, be, meta: (i, 0)),
        scratch_shapes=[pltpu.VMEM((D_MODEL, MOE_D_FF), BF16), pltpu.VMEM((D_MODEL, MOE_D_FF), BF16),
                        pltpu.VMEM((MOE_D_FF, D_MODEL), BF16)],
    )
    return pl.pallas_call(
        _ffn_body,
        grid_spec=grid_spec,
        out_shape=jax.ShapeDtypeStruct((n_slots, D_MODEL), F32),
        compiler_params=_params(1),
        name="ffn",
    )(block_e, meta, xs, w["w_expert_gate"], w["w_expert_up"], w["w_expert_down"])


def _combine_body(dest_ref, wt_ref, x_ref, ys_ref, gfin_ref, out_ref, buf0, buf1, sem, *, tm, final):
    bufs = (buf0, buf1)

    def issue(t, carry):
        for k in range(MOE_TOP_K):
            _row_copy(ys_ref, dest_ref[k, t], bufs[k], t, sem).start(priority=k)
        return carry

    def drain(t, carry):
        for k in range(MOE_TOP_K):
            _row_copy(ys_ref, dest_ref[k, t], bufs[k], t, sem).wait()
        return carry

    lax.fori_loop(0, tm, issue, 0, unroll=DMA_UNROLL)
    lax.fori_loop(0, tm, drain, 0, unroll=DMA_UNROLL)
    wt = wt_ref[...].T
    out = x_ref[...] + (wt[:, 0:1] * buf0[...] + wt[:, 1:2] * buf1[...])
    if final:
        out = _rms(out, gfin_ref[...])
    out_ref[...] = out


def _combine(x, ys, dest, wt, g_final, tm, final):
    t = x.shape[0]
    return pl.pallas_call(
        functools.partial(_combine_body, tm=tm, final=final),
        grid=(t // tm,),
        in_specs=[pl.BlockSpec((MOE_TOP_K, tm), lambda i: (0, i), memory_space=pltpu.SMEM),
                  pl.BlockSpec((SUBLANES, tm), lambda i: (0, i)),
                  pl.BlockSpec((tm, D_MODEL), lambda i: (i, 0)),
                  pl.BlockSpec(memory_space=pl.ANY),
                  _full((1, D_MODEL))],
        out_specs=pl.BlockSpec((tm, D_MODEL), lambda i: (i, 0)),
        out_shape=jax.ShapeDtypeStruct((t, D_MODEL), F32),
        scratch_shapes=[pltpu.VMEM((tm, D_MODEL), F32), pltpu.VMEM((tm, D_MODEL), F32), pltpu.SemaphoreType.DMA(())],
        compiler_params=_params(1, disable_bounds_checks=True),
        name="combine",
    )(dest, wt, x, ys, g_final)


def _moe(x, w, layer, g_final, final, tm, rows):
    t = x.shape[0]
    xn, eid, rank, wt, cnt = _route(x, w, tm)
    counts = cnt[:, 0].astype(jnp.int32)
    padded = (counts + rows - 1) // rows * rows
    pad_ends = jnp.cumsum(padded)
    pad_starts = pad_ends - padded
    n_blocks = (t * MOE_TOP_K) // rows + MOE_EXPERTS
    e_ids = jnp.arange(MOE_EXPERTS, dtype=jnp.int32)
    start_of = jnp.sum(jnp.where(eid[:MOE_TOP_K, :, None] == e_ids, pad_starts, 0), axis=-1)
    dest = start_of + rank[:MOE_TOP_K]
    block_lo = jnp.arange(n_blocks, dtype=jnp.int32) * rows
    block_e = jnp.minimum(jnp.sum((pad_ends[None, :] <= block_lo[:, None]).astype(jnp.int32), axis=1), MOE_EXPERTS - 1)
    meta = jnp.concatenate([pad_ends, padded, pad_ends[-1:] // rows]).astype(jnp.int32)
    xs = _dispatch(xn, dest, meta, n_blocks, rows, tm)
    ys = _ffn(xs, block_e, meta, w, layer, rows)
    return _combine(x, ys, dest, wt, g_final, tm, final)


def _expansion_tables():
    k = jnp.arange(SPLIT_PARTS * LANES, dtype=jnp.int32)[:, None] % LANES
    ch = jnp.arange(SSM_D_INNER, dtype=jnp.int32)[None, :] // SSM_HEAD_DIM
    col = jnp.arange(SSM_HEADS * LANES, dtype=jnp.int32)[None, :] // LANES
    return (k == ch).astype(BF16), (k == col).astype(BF16)


def _layer_weights(l, p):
    row = lambda a: a[l].reshape(1, -1)
    s1 = SSM_D_INNER
    s2 = s1 + SSM_CONV_DIM
    s3 = s2 + SSM_HEADS
    w_in = p["w_in"][l]
    pad_h = LANES - SSM_HEADS
    wr = jnp.zeros((SUBLANES + MOE_EXPERTS, D_MODEL), F32)
    wr = wr.at[:MOE_GROUPS].set(p["w_router_group"][l].T).at[SUBLANES:].set(p["w_router_expert"][l].T)
    br = jnp.zeros((SUBLANES + MOE_EXPERTS, 1), F32)
    br = br.at[:MOE_GROUPS, 0].set(p["b_router_group"][l]).at[SUBLANES:, 0].set(p["b_router_expert"][l].reshape(-1))
    expand_ch, expand_col = _expansion_tables()
    return {
        "norm_mix": row(p["norm_mix"]),
        "w_z": w_in[:, :s1].astype(BF16),
        "w_xbc": w_in[:, s1:s2].astype(BF16),
        "w_dt": jnp.pad(w_in[:, s2:s3], ((0, 0), (0, pad_h))).astype(BF16),
        "w_u": w_in[:, s3:s3 + GM_WIDTH].astype(BF16),
        "w_v": w_in[:, s3 + GM_WIDTH:].astype(BF16),
        "gm_ln_g": row(p["gm_ln_g"]), "gm_ln_b": row(p["gm_ln_b"]),
        "conv_w": p["conv_w"][l], "conv_b": row(p["conv_b"]),
        "dt_bias": jnp.pad(row(p["dt_bias"]), ((0, 0), (0, pad_h))),
        "a_log": jnp.pad(row(p["a_log"]), ((0, 0), (0, pad_h))),
        "d_skip": jnp.repeat(p["d_skip"][l], SSM_HEAD_DIM).reshape(1, -1),
        "ssm_norm": row(p["ssm_norm"]),
        "gm_ws": p["gm_ws"][l], "gm_bs_t": p["gm_bs"][l].T,
        "expand_ch": expand_ch, "expand_col": expand_col,
        "w_gate": p["w_gate"][l].astype(BF16), "b_gate": row(p["b_gate"]),
        "w_a_proj": p["w_a_proj"][l].astype(BF16), "w_b_proj": p["w_b_proj"][l].astype(BF16),
        "w_out": p["w_out"][l].astype(BF16),
        "norm_mem": row(p["norm_mem"]), "norm_memkv": row(p["norm_memkv"]),
        "w_mem_q": p["w_mem_q"][l].astype(BF16), "w_mem_k": p["w_mem_k"][l].astype(BF16),
        "w_mem_v": p["w_mem_v"][l].astype(BF16), "w_mem_o": p["w_mem_o"][l].astype(BF16),
        "norm_ffn": row(p["norm_ffn"]),
        "w_router_t": wr, "b_router_t": br,
        "w_expert_gate": p["w_expert_gate"], "w_expert_up": p["w_expert_up"], "w_expert_down": p["w_expert_down"],
    }


def kernel(x_prompt, x_sample, mem_prompt, state_ssm, state_conv, cache_mem_k, cache_mem_v, norm_mix, w_in, conv_w, conv_b, dt_bias, a_log, d_skip, ssm_norm, w_a_proj, gm_ln_g, gm_ln_b, gm_ws, gm_bs, w_b_proj, w_gate, b_gate, w_out, norm_mem, norm_memkv, w_mem_q, w_mem_k, w_mem_v, w_mem_o, norm_ffn, w_router_group, b_router_group, w_router_expert, b_router_expert, w_expert_gate, w_expert_up, w_expert_down, norm_final):
    p = dict(norm_mix=norm_mix, w_in=w_in, conv_w=conv_w, conv_b=conv_b, dt_bias=dt_bias, a_log=a_log, d_skip=d_skip,
             ssm_norm=ssm_norm, w_a_proj=w_a_proj, gm_ln_g=gm_ln_g, gm_ln_b=gm_ln_b, gm_ws=gm_ws, gm_bs=gm_bs,
             w_b_proj=w_b_proj, w_gate=w_gate, b_gate=b_gate, w_out=w_out, norm_mem=norm_mem, norm_memkv=norm_memkv,
             w_mem_q=w_mem_q, w_mem_k=w_mem_k, w_mem_v=w_mem_v, w_mem_o=w_mem_o, norm_ffn=norm_ffn,
             w_router_group=w_router_group, b_router_group=b_router_group, w_router_expert=w_router_expert,
             b_router_expert=b_router_expert, w_expert_gate=w_expert_gate, w_expert_up=w_expert_up,
             w_expert_down=w_expert_down)
    depth = w_in.shape[0]
    bp, lp, _ = x_prompt.shape
    bs, ls, _ = x_sample.shape
    assert lp % PROMPT_CHUNK == 0 and ls <= SAMPLE_ROWS
    g_final = norm_final.reshape(1, -1)
    hp = SSM_HEADS * SSM_HEAD_DIM
    state_shape = (SSM_HEADS, SSM_HEAD_DIM, SSM_STATE)
    state_ssm2 = state_ssm.reshape(depth, bs, hp, SSM_STATE)

    xp = x_prompt.reshape(bp * lp, D_MODEL)
    xs = jnp.pad(x_sample, ((0, 0), (0, SAMPLE_ROWS - ls), (0, 0))).reshape(bs * SAMPLE_ROWS, D_MODEL)
    tp, ts = bp * lp, bs * SAMPLE_ROWS
    tm_p, tm_s = min(256, tp), min(256, ts)
    tmm_p, tmm_s = min(512, tp), min(512, ts)
    lq_p = min(1024, lp)

    ssm_p, conv_p, mk_p, mv_p, ssm_s, conv_s, gv_s = [], [], [], [], [], [], []
    for l in range(depth):
        w = _layer_weights(l, p)
        final = l == depth - 1
        k_p, v_p, kb, vb = _memkv(mem_prompt, w)
        x2, h_new, c_new = _prompt_block(xp, kb, vb, w, bp, lp)
        xp = _moe(x2, w, l, g_final, final, tmm_p, MOE_ROWS_PROMPT)
        ssm_p.append(h_new.reshape((bp,) + state_shape))
        conv_p.append(c_new)
        mk_p.append(k_p)
        mv_p.append(v_p)
        z, xbc, dt, u, v = _inproj(xs, w, tm_s)
        stack_two = depth == 2 and l == 1
        y, sg, h_new, c_new = _mixer_sample(z, xbc, dt, u, v, w, bs, ls, state_ssm2, state_conv, l,
                                            ssm_s[0] if stack_two else None)
        x1, q = _merge(xs, y, sg, w, tm_s)
        x2 = _attn_sample(q, cache_mem_k, cache_mem_v, l, x1, w, bs, SAMPLE_ROWS)
        xs = _moe(x2, w, l, g_final, final, tmm_s, MOE_ROWS_SAMPLE)
        if stack_two:
            ssm_s = h_new.reshape((depth, bs) + state_shape)
        else:
            ssm_s.append(h_new)
        conv_s.append(c_new)
        gv_s.append(v.reshape(bs, SAMPLE_ROWS, GM_WIDTH)[:, :ls])
    if isinstance(ssm_s, list):
        ssm_s = jnp.stack(ssm_s).reshape((depth, bs) + state_shape)
    y_prompt = xp.reshape(bp, lp, D_MODEL)
    y_sample = xs.reshape(bs, SAMPLE_ROWS, D_MODEL)[:, :ls]
    return (y_prompt, y_sample, jnp.stack(ssm_p), jnp.stack(conv_p), jnp.stack(mk_p), jnp.stack(mv_p),
            ssm_s, jnp.stack(conv_s), jnp.stack(gv_s))
```

```python
import functools

import jax
import jax.numpy as jnp
from jax import lax
from jax.experimental import pallas as pl
from jax.experimental.pallas import tpu as pltpu

F32 = jnp.float32
BF16 = jnp.bfloat16
HIGHEST = lax.Precision.HIGHEST

NORM_EPS = 1e-6
D_MODEL = 1024
SSM_D_INNER = 1536
SSM_HEAD_DIM = 64
SSM_HEADS = 24
SSM_GROUPS = 4
SSM_HEADS_PER_GROUP = 6
SSM_STATE = 128
SSM_CONV = 4
SSM_CONV_DIM = 2560
SSM_GROUP_WIDTH = SSM_D_INNER // SSM_GROUPS
GM_WIDTH = 512
GM_GROUPS = 4
GM_GROUP_DIM = 128
MEM_HEADS = 4
MEM_HEAD_DIM = 256
MOE_GROUPS = 4
MOE_EXPERTS_PER_GROUP = 4
MOE_EXPERTS = 16
MOE_TOP_K = 2
MOE_D_FF = 512

LANES = 128
SUBLANES = 8
SAMPLE_ROWS = 8
SAMPLE_SEQS_PER_STEP = 4
ATTN_SEQS_PER_STEP = 2
PROMPT_CHUNK = 128
PROMPT_ROWS = 256
MOE_ROWS_PROMPT = 512
MOE_ROWS_SAMPLE = 128
SPLIT_PARTS = 3
DMA_UNROLL = 8
VMEM_LIMIT = 48 * 1024 * 1024
PROMPT_BLOCK_VMEM_LIMIT = 56 * 1024 * 1024

NT_DIMS = (((1,), (1,)), ((), ()))
TN_DIMS = (((0,), (0,)), ((), ()))


def _params(n_axes, **kw):
    return pltpu.CompilerParams(dimension_semantics=("arbitrary",) * n_axes, vmem_limit_bytes=VMEM_LIMIT, **kw)


def _rms(x, g):
    return x * lax.rsqrt(jnp.mean(x * x, axis=-1, keepdims=True) + NORM_EPS) * g


def _silu(x):
    return x * jax.nn.sigmoid(x)


def _gelu(x):
    return 0.5 * x * (1.0 + lax.erf(x * (2.0 ** -0.5)))


def _bdot(a, b):
    return jnp.dot(a.astype(BF16), b.astype(BF16), preferred_element_type=F32)


def _full(shape):
    n = len(shape)
    return pl.BlockSpec(shape, lambda *_: (0,) * n)


def _gm_uv(xn, wu_ref, wv_ref, lng_ref, lnb_ref):
    u = _gelu(jnp.dot(xn, wu_ref[...], preferred_element_type=F32))
    v = _gelu(jnp.dot(xn, wv_ref[...], preferred_element_type=F32))
    vc = v - jnp.mean(v, axis=-1, keepdims=True)
    var = jnp.mean(vc * vc, axis=-1, keepdims=True)
    return u, vc * lax.rsqrt(var + NORM_EPS) * lng_ref[...] + lnb_ref[...]


def _inproj_body(x_ref, g_ref, wz_ref, wx_ref, wdt_ref, wu_ref, wv_ref, lng_ref, lnb_ref,
                 z_ref, xbc_ref, dt_ref, u_ref, v_ref):
    xn = _rms(x_ref[...], g_ref[...]).astype(BF16)
    z_ref[...] = jnp.dot(xn, wz_ref[...], preferred_element_type=F32)
    xbc_ref[...] = jnp.dot(xn, wx_ref[...], preferred_element_type=F32)
    dt_ref[...] = jnp.dot(xn, wdt_ref[...], preferred_element_type=F32)
    u_ref[...], v_ref[...] = _gm_uv(xn, wu_ref, wv_ref, lng_ref, lnb_ref)


def _inproj(x, w, tm):
    t = x.shape[0]
    row = lambda n: pl.BlockSpec((tm, n), lambda i: (i, 0))
    outs = [SSM_D_INNER, SSM_CONV_DIM, LANES, GM_WIDTH, GM_WIDTH]
    return pl.pallas_call(
        _inproj_body,
        grid=(t // tm,),
        in_specs=[row(D_MODEL), _full((1, D_MODEL)), _full((D_MODEL, SSM_D_INNER)), _full((D_MODEL, SSM_CONV_DIM)),
                  _full((D_MODEL, LANES)), _full((D_MODEL, GM_WIDTH)), _full((D_MODEL, GM_WIDTH)),
                  _full((1, GM_WIDTH)), _full((1, GM_WIDTH))],
        out_specs=[row(n) for n in outs],
        out_shape=[jax.ShapeDtypeStruct((t, n), F32) for n in outs],
        compiler_params=_params(1),
        name="inproj",
    )(x, w["norm_mix"], w["w_z"], w["w_xbc"], w["w_dt"], w["w_u"], w["w_v"], w["gm_ln_g"], w["gm_ln_b"])


CONV_HIST = SSM_CONV - 1
CONV_BASE = SUBLANES


def _split_bf16(x):
    parts = []
    rest = x
    for _ in range(SPLIT_PARTS):
        piece = rest.astype(BF16)
        parts.append(piece)
        rest = rest - piece.astype(F32)
    return jnp.concatenate(parts, axis=1)


def _bf16_round(x):
    return x.astype(BF16).astype(F32)


def _conv_silu(win_ref, lc, cw_ref, cb_ref, base=CONV_BASE):
    xc = cb_ref[...]
    for k in range(SSM_CONV):
        lo = base - CONV_HIST + k
        xc = xc + _bf16_round(cw_ref[k:k + 1, :]) * win_ref[lo:lo + lc, :]
    return _silu(xc)


def _decay_cumsum(dt_raw, dtb_ref, alog_ref, tril, lv):
    lc = dt_raw.shape[0]
    dt = jax.nn.softplus(dt_raw + dtb_ref[...])
    if lv < lc:
        dt = jnp.where(lax.broadcasted_iota(jnp.int32, (lc, LANES), 0) < lv, dt, 0.0)
    a = -jnp.exp(alog_ref[...])
    acs = jnp.dot(tril.astype(F32), dt * a, precision=HIGHEST, preferred_element_type=F32)
    return dt, acs


def _gate_norm(y, z, nw_ref, out_ref):
    yz = y * _silu(z)
    for g in range(SSM_GROUPS):
        s0 = g * SSM_GROUP_WIDTH
        part = yz[:, s0:s0 + SSM_GROUP_WIDTH]
        ms = jnp.mean(part * part, axis=-1, keepdims=True)
        normed = part * lax.rsqrt(ms + NORM_EPS) * nw_ref[:, s0:s0 + SSM_GROUP_WIDTH]
        out_ref[:, s0:s0 + SSM_GROUP_WIDTH] = normed.astype(out_ref.dtype)


def _spatial_gate(u, v, tril, ws_ref, bst_ref, out_ref):
    lc = u.shape[0]
    for g in range(GM_GROUPS):
        k0 = g * GM_GROUP_DIM
        wc = jnp.where(tril, ws_ref[g, 0:lc, 0:lc], 0.0)
        s = _bdot(wc, v[:, k0:k0 + GM_GROUP_DIM]) + bst_ref[0:lc, g:g + 1]
        out_ref[:, k0:k0 + GM_GROUP_DIM] = (u[:, k0:k0 + GM_GROUP_DIM] * s).astype(out_ref.dtype)


def _tril(lc):
    return lax.broadcasted_iota(jnp.int32, (lc, lc), 0) >= lax.broadcasted_iota(jnp.int32, (lc, lc), 1)


def _ssd_wide(xc, dt, acs, tril, dsk_ref, ex_ref, ecol_ref, h_scr, y_scr, acst_scr):
    lc = xc.shape[0]
    acs_parts = _split_bf16(acs)
    wide = jnp.dot(jnp.concatenate([_split_bf16(dt), acs_parts], axis=0), ex_ref[...], preferred_element_type=F32)
    dtx = wide[:lc]
    acsx = wide[lc:]
    colb = jnp.dot(acs_parts, ecol_ref[...], preferred_element_type=F32)
    acst_scr[...] = acs.T
    c_col = jnp.exp(acst_scr[:, lc - 1:lc])
    xs = xc[:, :SSM_D_INNER]
    xdt = xs * dtx
    xdt_b = xdt.astype(BF16)
    xdte_b = (xdt * jnp.exp(acsx[lc - 1:lc, :] - acsx)).astype(BF16)
    e_acsx = jnp.exp(acsx)
    skip = dsk_ref[...] * xs
    first_head = lax.broadcasted_iota(jnp.int32, (lc, LANES), 1) < SSM_HEAD_DIM
    pair = 2 * SSM_HEAD_DIM
    for g in range(SSM_GROUPS):
        b0 = SSM_D_INNER + g * SSM_STATE
        c0 = SSM_D_INNER + SSM_GROUPS * SSM_STATE + g * SSM_STATE
        g0 = g * SSM_GROUP_WIDTH
        bm = xc[:, b0:b0 + SSM_STATE].astype(BF16)
        cm = xc[:, c0:c0 + SSM_STATE].astype(BF16)
        cb = lax.dot_general(cm, bm, NT_DIMS, preferred_element_type=F32)
        h_grp = h_scr[g0:g0 + SSM_GROUP_WIDTH, :]
        y_off = lax.dot_general(cm, h_grp.astype(BF16), NT_DIMS, preferred_element_type=F32)
        for j in range(SSM_HEADS_PER_GROUP // 2):
            h1 = g * SSM_HEADS_PER_GROUP + 2 * j
            p0 = h1 * SSM_HEAD_DIM
            atts = []
            for h in (h1, h1 + 1):
                seg = colb[:, h * LANES:(h + 1) * LANES] - acst_scr[h:h + 1, :]
                atts.append((cb * jnp.where(tril, jnp.exp(seg), 0.0)).astype(BF16))
            blk = xdt_b[:, p0:p0 + pair]
            zero = jnp.zeros_like(blk)
            rhs = jnp.concatenate([jnp.where(first_head, blk, zero), jnp.where(first_head, zero, blk)], axis=0)
            y_diag = jnp.dot(jnp.concatenate(atts, axis=1), rhs, preferred_element_type=F32)
            y_scr[:, p0:p0 + pair] = (y_diag + y_off[:, j * pair:(j + 1) * pair] * e_acsx[:, p0:p0 + pair]
                                      + skip[:, p0:p0 + pair])
        st = lax.dot_general(xdte_b[:, g0:g0 + SSM_GROUP_WIDTH], bm, TN_DIMS, preferred_element_type=F32)
        for r in range(SSM_HEADS_PER_GROUP):
            h = g * SSM_HEADS_PER_GROUP + r
            r0 = r * SSM_HEAD_DIM
            h_scr[g0 + r0:g0 + r0 + SSM_HEAD_DIM, :] = (h_grp[r0:r0 + SSM_HEAD_DIM, :] * c_col[h:h + 1, :]
                                                        + st[r0:r0 + SSM_HEAD_DIM, :])


PROMPT_BLOCK_WEIGHTS = ["norm_mix", "w_z", "w_xbc", "w_dt", "w_u", "w_v", "gm_ln_g", "gm_ln_b",
                        "conv_w", "conv_b", "dt_bias", "a_log", "d_skip", "ssm_norm", "gm_ws", "gm_bs_t",
                        "expand_ch", "expand_col", "w_gate", "b_gate", "w_a_proj", "w_b_proj", "w_out",
                        "norm_mem", "w_mem_q", "w_mem_o"]


def _prompt_block_body(x_ref, kb_ref, vb_ref,
                       gmix_ref, wz_ref, wx_ref, wdt_ref, wu_ref, wv_ref, lng_ref, lnb_ref,
                       cw_ref, cb_ref, dtb_ref, alog_ref, dsk_ref, nw_ref, ws_ref, bst_ref, ex_ref, ecol_ref,
                       wg_ref, bg_ref, wa_ref, wb_ref, wo_ref, gmem_ref, wq_ref, wmo_ref,
                       x2_ref, hout_ref, cout_ref,
                       xp_scr, y_scr, acst_scr, yn_scr, sg_scr, o_scr):
    lc = PROMPT_CHUNK
    rows = x_ref.shape[0]
    c = pl.program_id(1)

    @pl.when(c == 0)
    def _():
        xp_scr[CONV_BASE - CONV_HIST:CONV_BASE, :] = jnp.zeros((CONV_HIST, SSM_CONV_DIM), F32)
        hout_ref[...] = jnp.zeros(hout_ref.shape, F32)

    x = x_ref[...]
    xn = _rms(x, gmix_ref[...]).astype(BF16)
    xbc = jnp.dot(xn, wx_ref[...], preferred_element_type=F32)
    cout_ref[...] = xbc[rows - CONV_HIST:rows, :]
    xp_scr[CONV_BASE:CONV_BASE + rows, :] = _bf16_round(xbc)
    z = jnp.dot(xn, wz_ref[...], preferred_element_type=F32)
    dt_raw = jnp.dot(xn, wdt_ref[...], preferred_element_type=F32)
    u, v = _gm_uv(xn, wu_ref, wv_ref, lng_ref, lnb_ref)

    tril = _tril(lc)
    for sub in range(rows // lc):
        r0 = sub * lc
        xc = _conv_silu(xp_scr, lc, cw_ref, cb_ref, CONV_BASE + r0)
        dt, acs = _decay_cumsum(dt_raw[r0:r0 + lc], dtb_ref, alog_ref, tril, lc)
        _ssd_wide(xc, dt, acs, tril, dsk_ref, ex_ref, ecol_ref, hout_ref, y_scr.at[pl.ds(r0, lc)], acst_scr.at[sub])
        _spatial_gate(u[r0:r0 + lc], v[r0:r0 + lc], tril, ws_ref, bst_ref, sg_scr.at[pl.ds(r0, lc)])
    xp_scr[CONV_BASE - CONV_HIST:CONV_BASE, :] = xp_scr[CONV_BASE + rows - CONV_HIST:CONV_BASE + rows, :]

    _gate_norm(y_scr[...], z, nw_ref, yn_scr)
    x1 = _gated_merge(x, xn, yn_scr[...], sg_scr[...], wg_ref, bg_ref, wa_ref, wb_ref, wo_ref)

    q = _bdot(_rms(x1, gmem_ref[...]), wq_ref[...]).astype(BF16)
    scale = MEM_HEAD_DIM ** -0.5
    for h in range(MEM_HEADS):
        lo = h * MEM_HEAD_DIM
        s = lax.dot_general(q[:, lo:lo + MEM_HEAD_DIM], kb_ref[h], NT_DIMS, preferred_element_type=F32) * scale
        o_scr[:, lo:lo + MEM_HEAD_DIM] = jnp.dot(_softmax_rows(s).astype(BF16), vb_ref[h],
                                                 preferred_element_type=F32).astype(BF16)
    x2_ref[...] = x1 + _bdot(o_scr[...], wmo_ref[...])


def _mixer_weight_specs(w, names):
    return [_full(w[n].shape) for n in names], [w[n] for n in names]


def _prompt_block(x, kb, vb, w, bn, seq):
    lc = PROMPT_ROWS if seq % PROMPT_ROWS == 0 else PROMPT_CHUNK
    hp = SSM_HEADS * SSM_HEAD_DIM
    m = kb.shape[2]
    blk = pl.BlockSpec((None, lc, D_MODEL), lambda b, c: (b, c, 0))
    kv_spec = pl.BlockSpec((None, MEM_HEADS, m, MEM_HEAD_DIM), lambda b, c: (b, 0, 0, 0))
    once = lambda shape: pl.BlockSpec(shape, lambda b, c: (0,) * len(shape), pipeline_mode=pl.Buffered(1))
    x2, h_new, conv_new = pl.pallas_call(
        _prompt_block_body,
        grid=(bn, seq // lc),
        in_specs=[blk, kv_spec, kv_spec] + [once(w[n].shape) for n in PROMPT_BLOCK_WEIGHTS],
        out_specs=[blk,
                   pl.BlockSpec((None, hp, SSM_STATE), lambda b, c: (b, 0, 0)),
                   pl.BlockSpec((None, CONV_HIST, SSM_CONV_DIM), lambda b, c: (b, 0, 0))],
        out_shape=[jax.ShapeDtypeStruct((bn, seq, D_MODEL), F32),
                   jax.ShapeDtypeStruct((bn, hp, SSM_STATE), F32),
                   jax.ShapeDtypeStruct((bn, CONV_HIST, SSM_CONV_DIM), F32)],
        scratch_shapes=[pltpu.VMEM((CONV_BASE + lc, SSM_CONV_DIM), F32),
                        pltpu.VMEM((lc, SSM_D_INNER), F32),
                        pltpu.VMEM((lc // PROMPT_CHUNK, LANES, PROMPT_CHUNK), F32),
                        pltpu.VMEM((lc, SSM_D_INNER), BF16),
                        pltpu.VMEM((lc, GM_WIDTH), BF16),
                        pltpu.VMEM((lc, D_MODEL), BF16)],
        compiler_params=pltpu.CompilerParams(dimension_semantics=("arbitrary", "arbitrary"),
                                             vmem_limit_bytes=PROMPT_BLOCK_VMEM_LIMIT),
        name="prompt_block",
    )(x.reshape(bn, seq, D_MODEL), kb, vb, *[w[n] for n in PROMPT_BLOCK_WEIGHTS])
    return x2.reshape(bn * seq, D_MODEL), h_new, conv_new


def _ssd_short(xc, dt, acs, lv, dsk_ref, ex_ref, h_ref, hout_ref):
    lc = xc.shape[0]
    tot = acs[lc - 1:lc, :]
    e_acs = jnp.exp(acs)
    dt_end = dt * jnp.exp(tot - acs)
    c_col = jnp.exp(acs.T[:, lc - 1:lc])
    row = lax.broadcasted_iota(jnp.int32, (lc, LANES), 0)
    head = lax.broadcasted_iota(jnp.int32, (lc, LANES), 1)
    bms, cms, cbs = [], [], []
    for g in range(SSM_GROUPS):
        b0 = SSM_D_INNER + g * SSM_STATE
        c0 = SSM_D_INNER + SSM_GROUPS * SSM_STATE + g * SSM_STATE
        bms.append(xc[:, b0:b0 + SSM_STATE].astype(BF16))
        cms.append(xc[:, c0:c0 + SSM_STATE].astype(BF16))
        cbs.append(lax.dot_general(cms[g], bms[g], NT_DIMS, preferred_element_type=F32))
    coef = []
    for s in range(lv):
        decay = jnp.where(row >= s, jnp.exp(acs - acs[s:s + 1, :]), 0.0) * dt[s:s + 1, :]
        cb_s = jnp.zeros((lc, LANES), F32)
        for g in range(SSM_GROUPS):
            in_group = (head >= g * SSM_HEADS_PER_GROUP) & (head < (g + 1) * SSM_HEADS_PER_GROUP)
            cb_s = jnp.where(in_group, cbs[g][:, s:s + 1], cb_s)
        coef.append(decay * cb_s)
    wide = jnp.dot(_split_bf16(jnp.concatenate(coef + [e_acs, dt_end], axis=0)), ex_ref[...],
                   preferred_element_type=F32)
    xs = xc[:, :SSM_D_INNER]
    y = dsk_ref[...] * xs
    for s in range(lv):
        y = y + wide[s * lc:(s + 1) * lc] * xs[s:s + 1, :]
    e_acsx = wide[lv * lc:(lv + 1) * lc]
    xdte_b = (xs * wide[(lv + 1) * lc:]).astype(BF16)
    y_off = []
    for g in range(SSM_GROUPS):
        g0 = g * SSM_GROUP_WIDTH
        h_grp = h_ref[g0:g0 + SSM_GROUP_WIDTH, :]
        y_off.append(lax.dot_general(cms[g], h_grp.astype(BF16), NT_DIMS, preferred_element_type=F32))
        st = lax.dot_general(xdte_b[:, g0:g0 + SSM_GROUP_WIDTH], bms[g], TN_DIMS, preferred_element_type=F32)
        for r in range(SSM_HEADS_PER_GROUP):
            h = g * SSM_HEADS_PER_GROUP + r
            r0 = r * SSM_HEAD_DIM
            hout_ref[g0 + r0:g0 + r0 + SSM_HEAD_DIM, :] = (h_grp[r0:r0 + SSM_HEAD_DIM, :] * c_col[h:h + 1, :]
                                                           + st[r0:r0 + SSM_HEAD_DIM, :])
    return y + jnp.concatenate(y_off, axis=1) * e_acsx


def _mixer_sample_body(*refs, lv, sb, stack_prev):
    xbc_ref, dt_ref, z_ref, u_ref, v_ref, cprev_ref, h0_ref = refs[:7]
    i = 7
    if stack_prev:
        hprev_ref = refs[i]
        i += 1
    cw_ref, cb_ref, dtb_ref, alog_ref, dsk_ref, nw_ref, ws_ref, bst_ref, ex_ref = refs[i:i + 9]
    y_ref, sg_ref, hout_ref, cout_ref, xp_scr = refs[i + 9:]
    lc = SAMPLE_ROWS
    tril = _tril(lc)
    for s in range(sb):
        win = xp_scr.at[s]
        xbc = xbc_ref[s]
        win[CONV_BASE - CONV_HIST:CONV_BASE, :] = _bf16_round(cprev_ref[s])
        win[CONV_BASE:CONV_BASE + lc, :] = _bf16_round(xbc)
        xc = _conv_silu(win, lc, cw_ref, cb_ref)
        cout_ref[s] = xbc[lv - CONV_HIST:lv, :]
        dt, acs = _decay_cumsum(dt_ref[s], dtb_ref, alog_ref, tril, lv)
        if stack_prev:
            hout_ref[0, s] = hprev_ref[s]
            h_out = hout_ref.at[1, s]
        else:
            h_out = hout_ref.at[s]
        y = _ssd_short(xc, dt, acs, lv, dsk_ref, ex_ref, h0_ref.at[s], h_out)
        _gate_norm(y, z_ref[s], nw_ref, y_ref.at[s])
        _spatial_gate(u_ref[s], v_ref[s], tril, ws_ref, bst_ref, sg_ref.at[s])


def _mixer_sample(z, xbc, dt, u, v, w, bn, lv, state_ssm, state_conv, layer, h_prev_layer):
    lc = SAMPLE_ROWS
    sb = SAMPLE_SEQS_PER_STEP if bn % SAMPLE_SEQS_PER_STEP == 0 else 1
    stack_prev = h_prev_layer is not None
    hp = SSM_HEADS * SSM_HEAD_DIM
    r3 = lambda a: a.reshape(bn, lc, a.shape[-1])
    blk = lambda n: pl.BlockSpec((sb, lc, n), lambda b: (b, 0, 0))
    in_specs = [blk(SSM_CONV_DIM), blk(LANES), blk(SSM_D_INNER), blk(GM_WIDTH), blk(GM_WIDTH),
                pl.BlockSpec((None, sb, CONV_HIST, SSM_CONV_DIM), lambda b: (layer, b, 0, 0)),
                pl.BlockSpec((None, sb, hp, SSM_STATE), lambda b: (layer, b, 0, 0))]
    args = [r3(xbc), r3(dt), r3(z), r3(u), r3(v), state_conv, state_ssm]
    if stack_prev:
        in_specs.append(pl.BlockSpec((sb, hp, SSM_STATE), lambda b: (b, 0, 0)))
        args.append(h_prev_layer)
        h_spec = pl.BlockSpec((2, sb, hp, SSM_STATE), lambda b: (0, b, 0, 0))
        h_shape = jax.ShapeDtypeStruct((2, bn, hp, SSM_STATE), F32)
    else:
        h_spec = pl.BlockSpec((sb, hp, SSM_STATE), lambda b: (b, 0, 0))
        h_shape = jax.ShapeDtypeStruct((bn, hp, SSM_STATE), F32)
    w_specs, w_args = _mixer_weight_specs(w, ["conv_w", "conv_b", "dt_bias", "a_log", "d_skip", "ssm_norm", "gm_ws",
                                              "gm_bs_t", "expand_ch"])
    y, sg, h_new, conv_new = pl.pallas_call(
        functools.partial(_mixer_sample_body, lv=lv, sb=sb, stack_prev=stack_prev),
        grid=(bn // sb,),
        in_specs=in_specs + w_specs,
        out_specs=[blk(SSM_D_INNER), blk(GM_WIDTH), h_spec,
                   pl.BlockSpec((sb, CONV_HIST, SSM_CONV_DIM), lambda b: (b, 0, 0))],
        out_shape=[jax.ShapeDtypeStruct((bn, lc, SSM_D_INNER), F32), jax.ShapeDtypeStruct((bn, lc, GM_WIDTH), F32),
                   h_shape, jax.ShapeDtypeStruct((bn, CONV_HIST, SSM_CONV_DIM), F32)],
        scratch_shapes=[pltpu.VMEM((sb, CONV_BASE + lc, SSM_CONV_DIM), F32)],
        compiler_params=_params(1),
        name="mixer_sample",
    )(*args, *w_args)
    t = bn * lc
    return y.reshape(t, SSM_D_INNER), sg.reshape(t, GM_WIDTH), h_new, conv_new


def _gated_merge(x, xn, y, sg, wg_ref, bg_ref, wa_ref, wb_ref, wo_ref):
    gates = jax.nn.sigmoid(jnp.dot(xn, wg_ref[...], preferred_element_type=F32) + bg_ref[...])
    merged = gates[:, :D_MODEL] * _bdot(y, wa_ref[...]) + gates[:, D_MODEL:] * _bdot(sg, wb_ref[...])
    return x + _bdot(merged, wo_ref[...])


def _merge_body(x_ref, y_ref, sg_ref, gmix_ref, wg_ref, bg_ref, wa_ref, wb_ref, wo_ref, gmem_ref, wq_ref,
                x1_ref, q_ref):
    x = x_ref[...]
    xn = _rms(x, gmix_ref[...]).astype(BF16)
    x1 = _gated_merge(x, xn, y_ref[...], sg_ref[...], wg_ref, bg_ref, wa_ref, wb_ref, wo_ref)
    x1_ref[...] = x1
    q_ref[...] = _bdot(_rms(x1, gmem_ref[...]), wq_ref[...]).astype(BF16)


def _merge(x, y, sg, w, tm):
    t = x.shape[0]
    row = lambda n: pl.BlockSpec((tm, n), lambda i: (i, 0))
    return pl.pallas_call(
        _merge_body,
        grid=(t // tm,),
        in_specs=[row(D_MODEL), row(SSM_D_INNER), row(GM_WIDTH), _full((1, D_MODEL)),
                  _full((D_MODEL, 2 * D_MODEL)), _full((1, 2 * D_MODEL)), _full((SSM_D_INNER, D_MODEL)),
                  _full((GM_WIDTH, D_MODEL)), _full((D_MODEL, D_MODEL)), _full((1, D_MODEL)), _full((D_MODEL, D_MODEL))],
        out_specs=[row(D_MODEL), row(D_MODEL)],
        out_shape=[jax.ShapeDtypeStruct((t, D_MODEL), F32), jax.ShapeDtypeStruct((t, D_MODEL), BF16)],
        compiler_params=_params(1),
        name="merge",
    )(x, y, sg, w["norm_mix"], w["w_gate"], w["b_gate"], w["w_a_proj"], w["w_b_proj"], w["w_out"],
      w["norm_mem"], w["w_mem_q"])


def _memkv_body(m_ref, g_ref, wk_ref, wv_ref, k_ref, v_ref, kb_ref, vb_ref):
    mn = _rms(m_ref[...], g_ref[...]).astype(BF16)
    k = jnp.dot(mn, wk_ref[...], preferred_element_type=F32)
    v = jnp.dot(mn, wv_ref[...], preferred_element_type=F32)
    for h in range(MEM_HEADS):
        lo = h * MEM_HEAD_DIM
        k_ref[:, h, :] = k[:, lo:lo + MEM_HEAD_DIM]
        v_ref[:, h, :] = v[:, lo:lo + MEM_HEAD_DIM]
        kb_ref[h] = k[:, lo:lo + MEM_HEAD_DIM].astype(BF16)
        vb_ref[h] = v[:, lo:lo + MEM_HEAD_DIM].astype(BF16)


def _memkv(mem, w):
    bn, m, _ = mem.shape
    kv_spec = pl.BlockSpec((None, m, MEM_HEADS, MEM_HEAD_DIM), lambda b: (b, 0, 0, 0))
    kv_shape = jax.ShapeDtypeStruct((bn, m, MEM_HEADS, MEM_HEAD_DIM), F32)
    hb_spec = pl.BlockSpec((None, MEM_HEADS, m, MEM_HEAD_DIM), lambda b: (b, 0, 0, 0))
    hb_shape = jax.ShapeDtypeStruct((bn, MEM_HEADS, m, MEM_HEAD_DIM), BF16)
    return pl.pallas_call(
        _memkv_body,
        grid=(bn,),
        in_specs=[pl.BlockSpec((None, m, D_MODEL), lambda b: (b, 0, 0)), _full((1, D_MODEL)),
                  _full((D_MODEL, D_MODEL)), _full((D_MODEL, D_MODEL))],
        out_specs=[kv_spec, kv_spec, hb_spec, hb_spec],
        out_shape=[kv_shape, kv_shape, hb_shape, hb_shape],
        compiler_params=_params(1),
        name="memkv",
    )(mem, w["norm_memkv"], w["w_mem_k"], w["w_mem_v"])


def _softmax_rows(s):
    e = jnp.exp(s - jnp.max(s, axis=-1, keepdims=True))
    return e / jnp.sum(e, axis=-1, keepdims=True)


def _attn_sample_body(q_ref, k_ref, v_ref, x_ref, wo_ref, out_ref, o_scr):
    sb, rows, _ = q_ref.shape
    m = k_ref.shape[1]
    shape = (MEM_HEADS * rows, m * MEM_HEADS)
    own_head = (lax.broadcasted_iota(jnp.int32, shape, 0) // rows
                == lax.broadcasted_iota(jnp.int32, shape, 1) % MEM_HEADS)
    for i in range(sb):
        k2 = k_ref[i].reshape(m * MEM_HEADS, MEM_HEAD_DIM).astype(BF16)
        v2 = v_ref[i].reshape(m * MEM_HEADS, MEM_HEAD_DIM).astype(BF16)
        q = q_ref[i].astype(F32)
        q4 = jnp.concatenate([q[:, h * MEM_HEAD_DIM:(h + 1) * MEM_HEAD_DIM] for h in range(MEM_HEADS)], axis=0)
        s = lax.dot_general(q4.astype(BF16), k2, NT_DIMS, preferred_element_type=F32) * (MEM_HEAD_DIM ** -0.5)
        p = _softmax_rows(jnp.where(own_head, s, -jnp.inf))
        o4 = jnp.dot(p.astype(BF16), v2, preferred_element_type=F32)
        for h in range(MEM_HEADS):
            o_scr[i * rows:(i + 1) * rows, h * MEM_HEAD_DIM:(h + 1) * MEM_HEAD_DIM] = o4[h * rows:(h + 1) * rows, :]
    proj = _bdot(o_scr[...], wo_ref[...])
    for i in range(sb):
        out_ref[i] = x_ref[i] + proj[i * rows:(i + 1) * rows, :]


def _attn_sample(q, k5, v5, layer, x1, w, bn, rows):
    m = k5.shape[2]
    sb = ATTN_SEQS_PER_STEP if bn % ATTN_SEQS_PER_STEP == 0 else 1
    r3 = lambda a: a.reshape(bn, rows, D_MODEL)
    blk = pl.BlockSpec((sb, rows, D_MODEL), lambda b: (b, 0, 0))
    kv_spec = pl.BlockSpec((None, sb, m, MEM_HEADS, MEM_HEAD_DIM), lambda b: (layer, b, 0, 0, 0))
    out = pl.pallas_call(
        _attn_sample_body,
        grid=(bn // sb,),
        in_specs=[blk, kv_spec, kv_spec, blk, _full((D_MODEL, D_MODEL))],
        out_specs=blk,
        out_shape=jax.ShapeDtypeStruct((bn, rows, D_MODEL), F32),
        scratch_shapes=[pltpu.VMEM((sb * rows, D_MODEL), F32)],
        compiler_params=_params(1),
        name="attn_sample",
    )(r3(q), k5, v5, r3(x1), w["w_mem_o"])
    return out.reshape(bn * rows, D_MODEL)


def _first_max(vals):
    m = functools.reduce(jnp.maximum, vals)
    idx = jnp.full(m.shape, len(vals) - 1, jnp.int32)
    for j in range(len(vals) - 2, -1, -1):
        idx = jnp.where(vals[j] == m, j, idx)
    return m, idx


def _route_body(x_ref, g_ref, wr_ref, br_ref, xn_ref, eid_ref, rank_ref, wt_ref, cnt_ref, carry_scr, *, tm):
    @pl.when(pl.program_id(0) == 0)
    def _():
        carry_scr[...] = jnp.zeros(carry_scr.shape, F32)

    xn = _rms(x_ref[...], g_ref[...])
    xn_ref[...] = xn
    lg = lax.dot_general(wr_ref[...].astype(BF16), xn.astype(BF16), NT_DIMS, preferred_element_type=F32) + br_ref[...]
    grp = [lg[j:j + 1, :] for j in range(MOE_GROUPS)]
    gmax, gid = _first_max(grp)
    gw = 1.0 / functools.reduce(jnp.add, [jnp.exp(r - gmax) for r in grp])
    ex = [lg[SUBLANES + j:SUBLANES + j + 1, :] for j in range(MOE_EXPERTS)]
    sel = []
    for j in range(MOE_EXPERTS_PER_GROUP):
        pick = ex[(MOE_GROUPS - 1) * MOE_EXPERTS_PER_GROUP + j]
        for g in range(MOE_GROUPS - 2, -1, -1):
            pick = jnp.where(gid == g, ex[g * MOE_EXPERTS_PER_GROUP + j], pick)
        sel.append(pick)
    v1, i1 = _first_max(sel)
    rest = [jnp.where(i1 == j, -jnp.inf, sel[j]) for j in range(MOE_EXPERTS_PER_GROUP)]
    v2, i2 = _first_max(rest)
    e2 = jnp.exp(v2 - v1)
    den = 1.0 + e2
    eid1 = gid * MOE_EXPERTS_PER_GROUP + i1
    eid2 = gid * MOE_EXPERTS_PER_GROUP + i2

    e_iota = lax.broadcasted_iota(jnp.int32, (MOE_EXPERTS, tm), 0)
    m1 = e_iota == eid1
    m2 = e_iota == eid2
    onehot = jnp.where(m1, 1.0, 0.0) + jnp.where(m2, 1.0, 0.0)
    before = lax.broadcasted_iota(jnp.int32, (tm, tm), 0) < lax.broadcasted_iota(jnp.int32, (tm, tm), 1)
    ranks = _bdot(onehot, jnp.where(before, 1.0, 0.0)) + carry_scr[:, 0:1]
    r1 = jnp.sum(jnp.where(m1, ranks, 0.0), axis=0, keepdims=True)
    r2 = jnp.sum(jnp.where(m2, ranks, 0.0), axis=0, keepdims=True)
    carry_scr[...] = carry_scr[...] + jnp.sum(onehot, axis=1, keepdims=True)
    cnt_ref[...] = carry_scr[...]

    zeros_i = jnp.zeros((SUBLANES - MOE_TOP_K, tm), jnp.int32)
    eid_ref[0:1, :] = eid1
    eid_ref[1:2, :] = eid2
    eid_ref[MOE_TOP_K:, :] = zeros_i
    rank_ref[0:1, :] = r1.astype(jnp.int32)
    rank_ref[1:2, :] = r2.astype(jnp.int32)
    rank_ref[MOE_TOP_K:, :] = zeros_i
    wt_ref[0:1, :] = gw / den
    wt_ref[1:2, :] = gw * e2 / den
    wt_ref[MOE_TOP_K:, :] = jnp.zeros((SUBLANES - MOE_TOP_K, tm), F32)


def _route(x, w, tm):
    t = x.shape[0]
    lane_blk = pl.BlockSpec((SUBLANES, tm), lambda i: (0, i))
    nr = w["w_router_t"].shape[0]
    return pl.pallas_call(
        functools.partial(_route_body, tm=tm),
        grid=(t // tm,),
        in_specs=[pl.BlockSpec((tm, D_MODEL), lambda i: (i, 0)), _full((1, D_MODEL)), _full((nr, D_MODEL)), _full((nr, 1))],
        out_specs=[pl.BlockSpec((tm, D_MODEL), lambda i: (i, 0)), lane_blk, lane_blk, lane_blk,
                   _full((MOE_EXPERTS, LANES))],
        out_shape=[jax.ShapeDtypeStruct((t, D_MODEL), F32), jax.ShapeDtypeStruct((SUBLANES, t), jnp.int32),
                   jax.ShapeDtypeStruct((SUBLANES, t), jnp.int32), jax.ShapeDtypeStruct((SUBLANES, t), F32),
                   jax.ShapeDtypeStruct((MOE_EXPERTS, LANES), F32)],
        scratch_shapes=[pltpu.VMEM((MOE_EXPERTS, LANES), F32)],
        compiler_params=_params(1),
        name="route",
    )(x, w["norm_ffn"], w["w_router_t"], w["b_router_t"])


def _row_copy(src, src_row, dst, dst_row, sem):
    return pltpu.make_async_copy(src.at[pl.ds(src_row, 1), :], dst.at[pl.ds(dst_row, 1), :], sem)


META_ENDS = 0
META_SIZES = MOE_EXPERTS
META_USED = 2 * MOE_EXPERTS


def _dispatch_body(meta_ref, dest_ref, prev_dest_ref, xn_ref, out_ref, zbuf, stage, sem, zsem, *, tm, rows, n_blocks):
    i = pl.program_id(0)
    slot = i % 2

    @pl.when(i == 0)
    def _():
        zbuf[...] = jnp.zeros(zbuf.shape, F32)

        def zero_fills(act):
            for e in range(MOE_EXPERTS):
                @pl.when(meta_ref[META_SIZES + e] > 0)
                def _(e=e):
                    lo = pl.multiple_of(meta_ref[META_ENDS + e] - rows, rows)
                    act(pltpu.make_async_copy(zbuf, out_ref.at[pl.ds(lo, rows), :], zsem))
            for j in range(MOE_EXPERTS):
                @pl.when(meta_ref[META_USED] + j < n_blocks)
                def _(j=j):
                    lo = pl.multiple_of((meta_ref[META_USED] + j) * rows, rows)
                    act(pltpu.make_async_copy(zbuf, out_ref.at[pl.ds(lo, rows), :], zsem))

        zero_fills(lambda cp: cp.start())
        zero_fills(lambda cp: cp.wait())

    def copies(idx_ref, s, act):
        def body(t, carry):
            for k in range(MOE_TOP_K):
                act(_row_copy(stage.at[s], t, out_ref, idx_ref[k, t], sem.at[s]))
            return carry
        lax.fori_loop(0, tm, body, 0, unroll=DMA_UNROLL)

    stage[slot] = xn_ref[...]
    copies(dest_ref, slot, lambda cp: cp.start())

    @pl.when(i > 0)
    def _():
        copies(prev_dest_ref, 1 - slot, lambda cp: cp.wait())

    @pl.when(i == pl.num_programs(0) - 1)
    def _():
        copies(dest_ref, slot, lambda cp: cp.wait())


def _dispatch(xn, dest, meta, n_blocks, rows, tm):
    t = xn.shape[0]
    grid_spec = pltpu.PrefetchScalarGridSpec(
        num_scalar_prefetch=1,
        grid=(t // tm,),
        in_specs=[pl.BlockSpec((MOE_TOP_K, tm), lambda i, meta: (0, i), memory_space=pltpu.SMEM),
                  pl.BlockSpec((MOE_TOP_K, tm), lambda i, meta: (0, jnp.maximum(i - 1, 0)), memory_space=pltpu.SMEM),
                  pl.BlockSpec((tm, D_MODEL), lambda i, meta: (i, 0))],
        out_specs=pl.BlockSpec(memory_space=pl.ANY),
        scratch_shapes=[pltpu.VMEM((rows, D_MODEL), F32), pltpu.VMEM((2, tm, D_MODEL), F32),
                        pltpu.SemaphoreType.DMA((2,)), pltpu.SemaphoreType.DMA(())],
    )
    return pl.pallas_call(
        functools.partial(_dispatch_body, tm=tm, rows=rows, n_blocks=n_blocks),
        grid_spec=grid_spec,
        out_shape=jax.ShapeDtypeStruct((n_blocks * rows, D_MODEL), F32),
        compiler_params=_params(1, disable_bounds_checks=True),
        name="dispatch",
    )(meta, dest, dest, xn)


def _ffn_body(be_ref, meta_ref, x_ref, wg_ref, wu_ref, wd_ref, o_ref, wg_scr, wu_scr, wd_scr):
    i = pl.program_id(0)
    last = jnp.maximum(meta_ref[META_USED] - 1, 0)
    used = i < meta_ref[META_USED]
    e_now = be_ref[jnp.minimum(i, last)]
    e_before = be_ref[jnp.minimum(jnp.maximum(i - 1, 0), last)]

    @pl.when(jnp.logical_or(i == 0, e_now != e_before))
    def _():
        wg_scr[...] = wg_ref[...].astype(BF16)
        wu_scr[...] = wu_ref[...].astype(BF16)
        wd_scr[...] = wd_ref[...].astype(BF16)

    @pl.when(used)
    def _():
        x = x_ref[...].astype(BF16)
        hid = (_silu(jnp.dot(x, wg_scr[...], preferred_element_type=F32))
               * jnp.dot(x, wu_scr[...], preferred_element_type=F32))
        o_ref[...] = jnp.dot(hid.astype(BF16), wd_scr[...], preferred_element_type=F32)

    @pl.when(jnp.logical_not(used))
    def _():
        o_ref[...] = jnp.zeros(o_ref.shape, F32)


def _ffn(xs, block_e, meta, w, layer, rows):
    n_slots = xs.shape[0]
    last_used = lambda i, meta: jnp.minimum(i, jnp.maximum(meta[META_USED] - 1, 0))
    blk_i = lambda i, be, meta: (last_used(i, meta), 0)
    exp_i = lambda i, be, meta: (layer, be[last_used(i, meta)], 0, 0)
    grid_spec = pltpu.PrefetchScalarGridSpec(
        num_scalar_prefetch=2,
        grid=(n_slots // rows,),
        in_specs=[pl.BlockSpec((rows, D_MODEL), blk_i),
                  pl.BlockSpec((None, None, D_MODEL, MOE_D_FF), exp_i),
                  pl.BlockSpec((None, None, D_MODEL, MOE_D_FF), exp_i),
                  pl.BlockSpec((None, None, MOE_D_FF, D_MODEL), exp_i)],
        out_specs=pl.BlockSpec((rows, D_MODEL), lambda i, be, meta: (i, 0)),
        scratch_shapes=[pltpu.VMEM((D_MODEL, MOE_D_FF), BF16), pltpu.VMEM((D_MODEL, MOE_D_FF), BF16),
                        pltpu.VMEM((MOE_D_FF, D_MODEL), BF16)],
    )
    return pl.pallas_call(
        _ffn_body,
        grid_spec=grid_spec,
        out_shape=jax.ShapeDtypeStruct((n_slots, D_MODEL), F32),
        compiler_params=_params(1),
        name="ffn",
    )(block_e, meta, xs, w["w_expert_gate"], w["w_expert_up"], w["w_expert_down"])


def _combine_body(dest_ref, prev_dest_ref, wt_ref, x_ref, ys_ref, gfin_ref, out_ref, bufs, sem, *, tm, final):
    i = pl.program_id(0)
    n_tiles = pl.num_programs(0) - 1
    slot = i % 2

    def copies(idx_ref, s, act):
        def body(t, carry):
            for k in range(MOE_TOP_K):
                act(_row_copy(ys_ref, idx_ref[k, t], bufs.at[s, k], t, sem.at[s]))
            return carry
        lax.fori_loop(0, tm, body, 0, unroll=DMA_UNROLL)

    @pl.when(i < n_tiles)
    def _():
        copies(dest_ref, slot, lambda cp: cp.start())

    @pl.when(i > 0)
    def _():
        copies(prev_dest_ref, 1 - slot, lambda cp: cp.wait())
        wt = wt_ref[...].T
        out = x_ref[...] + (wt[:, 0:1] * bufs[1 - slot, 0] + wt[:, 1:2] * bufs[1 - slot, 1])
        if final:
            out = _rms(out, gfin_ref[...])
        out_ref[...] = out


def _combine(x, ys, dest, wt, g_final, tm, final):
    t = x.shape[0]
    n_tiles = t // tm
    cur = lambda i: jnp.minimum(i, n_tiles - 1)
    prev = lambda i: jnp.maximum(i - 1, 0)
    return pl.pallas_call(
        functools.partial(_combine_body, tm=tm, final=final),
        grid=(n_tiles + 1,),
        in_specs=[pl.BlockSpec((MOE_TOP_K, tm), lambda i: (0, cur(i)), memory_space=pltpu.SMEM),
                  pl.BlockSpec((MOE_TOP_K, tm), lambda i: (0, prev(i)), memory_space=pltpu.SMEM),
                  pl.BlockSpec((SUBLANES, tm), lambda i: (0, prev(i))),
                  pl.BlockSpec((tm, D_MODEL), lambda i: (prev(i), 0)),
                  pl.BlockSpec(memory_space=pl.ANY),
                  _full((1, D_MODEL))],
        out_specs=pl.BlockSpec((tm, D_MODEL), lambda i: (prev(i), 0)),
        out_shape=jax.ShapeDtypeStruct((t, D_MODEL), F32),
        scratch_shapes=[pltpu.VMEM((2, MOE_TOP_K, tm, D_MODEL), F32), pltpu.SemaphoreType.DMA((2,))],
        compiler_params=_params(1, disable_bounds_checks=True),
        name="combine",
    )(dest, dest, wt, x, ys, g_final)


def _moe(x, w, layer, g_final, final, tm, rows):
    t = x.shape[0]
    xn, eid, rank, wt, cnt = _route(x, w, tm)
    counts = cnt[:, 0].astype(jnp.int32)
    padded = (counts + rows - 1) // rows * rows
    pad_ends = jnp.cumsum(padded)
    pad_starts = pad_ends - padded
    n_blocks = (t * MOE_TOP_K) // rows + MOE_EXPERTS
    e_ids = jnp.arange(MOE_EXPERTS, dtype=jnp.int32)
    start_of = jnp.sum(jnp.where(eid[:MOE_TOP_K, :, None] == e_ids, pad_starts, 0), axis=-1)
    dest = start_of + rank[:MOE_TOP_K]
    block_lo = jnp.arange(n_blocks, dtype=jnp.int32) * rows
    block_e = jnp.minimum(jnp.sum((pad_ends[None, :] <= block_lo[:, None]).astype(jnp.int32), axis=1), MOE_EXPERTS - 1)
    meta = jnp.concatenate([pad_ends, padded, pad_ends[-1:] // rows]).astype(jnp.int32)
    xs = _dispatch(xn, dest, meta, n_blocks, rows, tm)
    ys = _ffn(xs, block_e, meta, w, layer, rows)
    return _combine(x, ys, dest, wt, g_final, tm, final)


def _expansion_tables():
    k = jnp.arange(SPLIT_PARTS * LANES, dtype=jnp.int32)[:, None] % LANES
    ch = jnp.arange(SSM_D_INNER, dtype=jnp.int32)[None, :] // SSM_HEAD_DIM
    col = jnp.arange(SSM_HEADS * LANES, dtype=jnp.int32)[None, :] // LANES
    return (k == ch).astype(BF16), (k == col).astype(BF16)


def _layer_weights(l, p):
    row = lambda a: a[l].reshape(1, -1)
    s1 = SSM_D_INNER
    s2 = s1 + SSM_CONV_DIM
    s3 = s2 + SSM_HEADS
    w_in = p["w_in"][l]
    pad_h = LANES - SSM_HEADS
    wr = jnp.zeros((SUBLANES + MOE_EXPERTS, D_MODEL), F32)
    wr = wr.at[:MOE_GROUPS].set(p["w_router_group"][l].T).at[SUBLANES:].set(p["w_router_expert"][l].T)
    br = jnp.zeros((SUBLANES + MOE_EXPERTS, 1), F32)
    br = br.at[:MOE_GROUPS, 0].set(p["b_router_group"][l]).at[SUBLANES:, 0].set(p["b_router_expert"][l].reshape(-1))
    expand_ch, expand_col = _expansion_tables()
    return {
        "norm_mix": row(p["norm_mix"]),
        "w_z": w_in[:, :s1].astype(BF16),
        "w_xbc": w_in[:, s1:s2].astype(BF16),
        "w_dt": jnp.pad(w_in[:, s2:s3], ((0, 0), (0, pad_h))).astype(BF16),
        "w_u": w_in[:, s3:s3 + GM_WIDTH].astype(BF16),
        "w_v": w_in[:, s3 + GM_WIDTH:].astype(BF16),
        "gm_ln_g": row(p["gm_ln_g"]), "gm_ln_b": row(p["gm_ln_b"]),
        "conv_w": p["conv_w"][l], "conv_b": row(p["conv_b"]),
        "dt_bias": jnp.pad(row(p["dt_bias"]), ((0, 0), (0, pad_h))),
        "a_log": jnp.pad(row(p["a_log"]), ((0, 0), (0, pad_h))),
        "d_skip": jnp.repeat(p["d_skip"][l], SSM_HEAD_DIM).reshape(1, -1),
        "ssm_norm": row(p["ssm_norm"]),
        "gm_ws": p["gm_ws"][l], "gm_bs_t": p["gm_bs"][l].T,
        "expand_ch": expand_ch, "expand_col": expand_col,
        "w_gate": p["w_gate"][l].astype(BF16), "b_gate": row(p["b_gate"]),
        "w_a_proj": p["w_a_proj"][l].astype(BF16), "w_b_proj": p["w_b_proj"][l].astype(BF16),
        "w_out": p["w_out"][l].astype(BF16),
        "norm_mem": row(p["norm_mem"]), "norm_memkv": row(p["norm_memkv"]),
        "w_mem_q": p["w_mem_q"][l].astype(BF16), "w_mem_k": p["w_mem_k"][l].astype(BF16),
        "w_mem_v": p["w_mem_v"][l].astype(BF16), "w_mem_o": p["w_mem_o"][l].astype(BF16),
        "norm_ffn": row(p["norm_ffn"]),
        "w_router_t": wr, "b_router_t": br,
        "w_expert_gate": p["w_expert_gate"], "w_expert_up": p["w_expert_up"], "w_expert_down": p["w_expert_down"],
    }


def kernel(x_prompt, x_sample, mem_prompt, state_ssm, state_conv, cache_mem_k, cache_mem_v, norm_mix, w_in, conv_w, conv_b, dt_bias, a_log, d_skip, ssm_norm, w_a_proj, gm_ln_g, gm_ln_b, gm_ws, gm_bs, w_b_proj, w_gate, b_gate, w_out, norm_mem, norm_memkv, w_mem_q, w_mem_k, w_mem_v, w_mem_o, norm_ffn, w_router_group, b_router_group, w_router_expert, b_router_expert, w_expert_gate, w_expert_up, w_expert_down, norm_final):
    p = dict(norm_mix=norm_mix, w_in=w_in, conv_w=conv_w, conv_b=conv_b, dt_bias=dt_bias, a_log=a_log, d_skip=d_skip,
             ssm_norm=ssm_norm, w_a_proj=w_a_proj, gm_ln_g=gm_ln_g, gm_ln_b=gm_ln_b, gm_ws=gm_ws, gm_bs=gm_bs,
             w_b_proj=w_b_proj, w_gate=w_gate, b_gate=b_gate, w_out=w_out, norm_mem=norm_mem, norm_memkv=norm_memkv,
             w_mem_q=w_mem_q, w_mem_k=w_mem_k, w_mem_v=w_mem_v, w_mem_o=w_mem_o, norm_ffn=norm_ffn,
             w_router_group=w_router_group, b_router_group=b_router_group, w_router_expert=w_router_expert,
             b_router_expert=b_router_expert, w_expert_gate=w_expert_gate, w_expert_up=w_expert_up,
             w_expert_down=w_expert_down)
    depth = w_in.shape[0]
    bp, lp, _ = x_prompt.shape
    bs, ls, _ = x_sample.shape
    assert lp % PROMPT_CHUNK == 0 and ls <= SAMPLE_ROWS
    g_final = norm_final.reshape(1, -1)
    hp = SSM_HEADS * SSM_HEAD_DIM
    state_shape = (SSM_HEADS, SSM_HEAD_DIM, SSM_STATE)
    state_ssm2 = state_ssm.reshape(depth, bs, hp, SSM_STATE)

    xp = x_prompt.reshape(bp * lp, D_MODEL)
    xs = jnp.pad(x_sample, ((0, 0), (0, SAMPLE_ROWS - ls), (0, 0))).reshape(bs * SAMPLE_ROWS, D_MODEL)
    tp, ts = bp * lp, bs * SAMPLE_ROWS
    tm_p, tm_s = min(256, tp), min(256, ts)
    tmm_p, tmm_s = min(512, tp), min(512, ts)
    lq_p = min(1024, lp)

    ssm_p, conv_p, mk_p, mv_p, ssm_s, conv_s, gv_s = [], [], [], [], [], [], []
    for l in range(depth):
        w = _layer_weights(l, p)
        final = l == depth - 1
        k_p, v_p, kb, vb = _memkv(mem_prompt, w)
        x2, h_new, c_new = _prompt_block(xp, kb, vb, w, bp, lp)
        xp = _moe(x2, w, l, g_final, final, tmm_p, MOE_ROWS_PROMPT)
        ssm_p.append(h_new.reshape((bp,) + state_shape))
        conv_p.append(c_new)
        mk_p.append(k_p)
        mv_p.append(v_p)
        z, xbc, dt, u, v = _inproj(xs, w, tm_s)
        stack_two = depth == 2 and l == 1
        y, sg, h_new, c_new = _mixer_sample(z, xbc, dt, u, v, w, bs, ls, state_ssm2, state_conv, l,
                                            ssm_s[0] if stack_two else None)
        x1, q = _merge(xs, y, sg, w, tm_s)
        x2 = _attn_sample(q, cache_mem_k, cache_mem_v, l, x1, w, bs, SAMPLE_ROWS)
        xs = _moe(x2, w, l, g_final, final, tmm_s, MOE_ROWS_SAMPLE)
        if stack_two:
            ssm_s = h_new.reshape((depth, bs) + state_shape)
        else:
            ssm_s.append(h_new)
        conv_s.append(c_new)
        gv_s.append(v.reshape(bs, SAMPLE_ROWS, GM_WIDTH)[:, :ls])
    if isinstance(ssm_s, list):
        ssm_s = jnp.stack(ssm_s).reshape((depth, bs) + state_shape)
    y_prompt = xp.reshape(bp, lp, D_MODEL)
    y_sample = xs.reshape(bs, SAMPLE_ROWS, D_MODEL)[:, :ls]
    return (y_prompt, y_sample, jnp.stack(ssm_p), jnp.stack(conv_p), jnp.stack(mk_p), jnp.stack(mv_p),
            ssm_s, jnp.stack(conv_s), jnp.stack(gv_s))
```

```python
import functools

import jax
import jax.numpy as jnp
from jax import lax
from jax.experimental import pallas as pl
from jax.experimental.pallas import tpu as pltpu

F32 = jnp.float32
BF16 = jnp.bfloat16
HIGHEST = lax.Precision.HIGHEST

NORM_EPS = 1e-6
D_MODEL = 1024
SSM_D_INNER = 1536
SSM_HEAD_DIM = 64
SSM_HEADS = 24
SSM_GROUPS = 4
SSM_HEADS_PER_GROUP = 6
SSM_STATE = 128
SSM_CONV = 4
SSM_CONV_DIM = 2560
SSM_GROUP_WIDTH = SSM_D_INNER // SSM_GROUPS
GM_WIDTH = 512
GM_GROUPS = 4
GM_GROUP_DIM = 128
MEM_HEADS = 4
MEM_HEAD_DIM = 256
MOE_GROUPS = 4
MOE_EXPERTS_PER_GROUP = 4
MOE_EXPERTS = 16
MOE_TOP_K = 2
MOE_D_FF = 512

LANES = 128
SUBLANES = 8
SAMPLE_ROWS = 8
SAMPLE_SEQS_PER_STEP = 4
ATTN_SEQS_PER_STEP = 2
PROMPT_CHUNK = 128
PROMPT_ROWS = 256
MOE_ROWS_PROMPT = 512
MOE_ROWS_SAMPLE = 128
SPLIT_PARTS = 3
DMA_UNROLL = 8
VMEM_LIMIT = 48 * 1024 * 1024
PROMPT_BLOCK_VMEM_LIMIT = 56 * 1024 * 1024

NT_DIMS = (((1,), (1,)), ((), ()))
TN_DIMS = (((0,), (0,)), ((), ()))


def _params(n_axes, **kw):
    return pltpu.CompilerParams(dimension_semantics=("arbitrary",) * n_axes, vmem_limit_bytes=VMEM_LIMIT, **kw)


def _rms(x, g):
    return x * lax.rsqrt(jnp.mean(x * x, axis=-1, keepdims=True) + NORM_EPS) * g


def _silu(x):
    return x * jax.nn.sigmoid(x)


def _gelu(x):
    return 0.5 * x * (1.0 + lax.erf(x * (2.0 ** -0.5)))


def _bdot(a, b):
    return jnp.dot(a.astype(BF16), b.astype(BF16), preferred_element_type=F32)


def _full(shape):
    n = len(shape)
    return pl.BlockSpec(shape, lambda *_: (0,) * n)


def _gm_uv(xn, wu_ref, wv_ref, lng_ref, lnb_ref):
    u = _gelu(jnp.dot(xn, wu_ref[...], preferred_element_type=F32))
    v = _gelu(jnp.dot(xn, wv_ref[...], preferred_element_type=F32))
    vc = v - jnp.mean(v, axis=-1, keepdims=True)
    var = jnp.mean(vc * vc, axis=-1, keepdims=True)
    return u, vc * lax.rsqrt(var + NORM_EPS) * lng_ref[...] + lnb_ref[...]


def _inproj_body(x_ref, g_ref, wz_ref, wx_ref, wdt_ref, wu_ref, wv_ref, lng_ref, lnb_ref,
                 z_ref, xbc_ref, dt_ref, u_ref, v_ref):
    xn = _rms(x_ref[...], g_ref[...]).astype(BF16)
    z_ref[...] = jnp.dot(xn, wz_ref[...], preferred_element_type=F32)
    xbc_ref[...] = jnp.dot(xn, wx_ref[...], preferred_element_type=F32)
    dt_ref[...] = jnp.dot(xn, wdt_ref[...], preferred_element_type=F32)
    u_ref[...], v_ref[...] = _gm_uv(xn, wu_ref, wv_ref, lng_ref, lnb_ref)


def _inproj(x, w, tm):
    t = x.shape[0]
    row = lambda n: pl.BlockSpec((tm, n), lambda i: (i, 0))
    outs = [SSM_D_INNER, SSM_CONV_DIM, LANES, GM_WIDTH, GM_WIDTH]
    return pl.pallas_call(
        _inproj_body,
        grid=(t // tm,),
        in_specs=[row(D_MODEL), _full((1, D_MODEL)), _full((D_MODEL, SSM_D_INNER)), _full((D_MODEL, SSM_CONV_DIM)),
                  _full((D_MODEL, LANES)), _full((D_MODEL, GM_WIDTH)), _full((D_MODEL, GM_WIDTH)),
                  _full((1, GM_WIDTH)), _full((1, GM_WIDTH))],
        out_specs=[row(n) for n in outs],
        out_shape=[jax.ShapeDtypeStruct((t, n), F32) for n in outs],
        compiler_params=_params(1),
        name="inproj",
    )(x, w["norm_mix"], w["w_z"], w["w_xbc"], w["w_dt"], w["w_u"], w["w_v"], w["gm_ln_g"], w["gm_ln_b"])


CONV_HIST = SSM_CONV - 1
CONV_BASE = SUBLANES


def _split_bf16(x):
    parts = []
    rest = x
    for _ in range(SPLIT_PARTS):
        piece = rest.astype(BF16)
        parts.append(piece)
        rest = rest - piece.astype(F32)
    return jnp.concatenate(parts, axis=1)


def _bf16_round(x):
    return x.astype(BF16).astype(F32)


def _conv_silu(win_ref, lc, cw_ref, cb_ref, base=CONV_BASE):
    xc = cb_ref[...]
    for k in range(SSM_CONV):
        lo = base - CONV_HIST + k
        xc = xc + _bf16_round(cw_ref[k:k + 1, :]) * win_ref[lo:lo + lc, :]
    return _silu(xc)


def _decay_cumsum(dt_raw, dtb_ref, alog_ref, tril, lv):
    lc = dt_raw.shape[0]
    dt = jax.nn.softplus(dt_raw + dtb_ref[...])
    if lv < lc:
        dt = jnp.where(lax.broadcasted_iota(jnp.int32, (lc, LANES), 0) < lv, dt, 0.0)
    a = -jnp.exp(alog_ref[...])
    acs = jnp.dot(tril.astype(F32), dt * a, precision=HIGHEST, preferred_element_type=F32)
    return dt, acs


def _gate_norm(y, z, nw_ref, out_ref):
    yz = y * _silu(z)
    for g in range(SSM_GROUPS):
        s0 = g * SSM_GROUP_WIDTH
        part = yz[:, s0:s0 + SSM_GROUP_WIDTH]
        ms = jnp.mean(part * part, axis=-1, keepdims=True)
        normed = part * lax.rsqrt(ms + NORM_EPS) * nw_ref[:, s0:s0 + SSM_GROUP_WIDTH]
        out_ref[:, s0:s0 + SSM_GROUP_WIDTH] = normed.astype(out_ref.dtype)


def _spatial_gate(u, v, tril, ws_ref, bst_ref, out_ref):
    lc = u.shape[0]
    for g in range(GM_GROUPS):
        k0 = g * GM_GROUP_DIM
        wc = jnp.where(tril, ws_ref[g, 0:lc, 0:lc], 0.0)
        s = _bdot(wc, v[:, k0:k0 + GM_GROUP_DIM]) + bst_ref[0:lc, g:g + 1]
        out_ref[:, k0:k0 + GM_GROUP_DIM] = (u[:, k0:k0 + GM_GROUP_DIM] * s).astype(out_ref.dtype)


def _tril(lc):
    return lax.broadcasted_iota(jnp.int32, (lc, lc), 0) >= lax.broadcasted_iota(jnp.int32, (lc, lc), 1)


def _ssd_wide(xc, dt, acs, tril, dsk_ref, ex_ref, ecol_ref, h_scr, y_scr, acst_scr):
    lc = xc.shape[0]
    acs_parts = _split_bf16(acs)
    wide = jnp.dot(jnp.concatenate([_split_bf16(dt), acs_parts], axis=0), ex_ref[...], preferred_element_type=F32)
    dtx = wide[:lc]
    acsx = wide[lc:]
    colb = jnp.dot(acs_parts, ecol_ref[...], preferred_element_type=F32)
    acst_scr[...] = acs.T
    c_col = jnp.exp(acst_scr[:, lc - 1:lc])
    xs = xc[:, :SSM_D_INNER]
    xdt = xs * dtx
    xdt_b = xdt.astype(BF16)
    xdte_b = (xdt * jnp.exp(acsx[lc - 1:lc, :] - acsx)).astype(BF16)
    e_acsx = jnp.exp(acsx)
    skip = dsk_ref[...] * xs
    first_head = lax.broadcasted_iota(jnp.int32, (lc, LANES), 1) < SSM_HEAD_DIM
    pair = 2 * SSM_HEAD_DIM
    for g in range(SSM_GROUPS):
        b0 = SSM_D_INNER + g * SSM_STATE
        c0 = SSM_D_INNER + SSM_GROUPS * SSM_STATE + g * SSM_STATE
        g0 = g * SSM_GROUP_WIDTH
        bm = xc[:, b0:b0 + SSM_STATE].astype(BF16)
        cm = xc[:, c0:c0 + SSM_STATE].astype(BF16)
        cb = lax.dot_general(cm, bm, NT_DIMS, preferred_element_type=F32)
        h_grp = h_scr[g0:g0 + SSM_GROUP_WIDTH, :]
        y_off = lax.dot_general(cm, h_grp.astype(BF16), NT_DIMS, preferred_element_type=F32)
        for j in range(SSM_HEADS_PER_GROUP // 2):
            h1 = g * SSM_HEADS_PER_GROUP + 2 * j
            p0 = h1 * SSM_HEAD_DIM
            atts = []
            for h in (h1, h1 + 1):
                seg = colb[:, h * LANES:(h + 1) * LANES] - acst_scr[h:h + 1, :]
                atts.append((cb * jnp.where(tril, jnp.exp(seg), 0.0)).astype(BF16))
            blk = xdt_b[:, p0:p0 + pair]
            zero = jnp.zeros_like(blk)
            rhs = jnp.concatenate([jnp.where(first_head, blk, zero), jnp.where(first_head, zero, blk)], axis=0)
            y_diag = jnp.dot(jnp.concatenate(atts, axis=1), rhs, preferred_element_type=F32)
            y_scr[:, p0:p0 + pair] = (y_diag + y_off[:, j * pair:(j + 1) * pair] * e_acsx[:, p0:p0 + pair]
                                      + skip[:, p0:p0 + pair])
        st = lax.dot_general(xdte_b[:, g0:g0 + SSM_GROUP_WIDTH], bm, TN_DIMS, preferred_element_type=F32)
        for r in range(SSM_HEADS_PER_GROUP):
            h = g * SSM_HEADS_PER_GROUP + r
            r0 = r * SSM_HEAD_DIM
            h_scr[g0 + r0:g0 + r0 + SSM_HEAD_DIM, :] = (h_grp[r0:r0 + SSM_HEAD_DIM, :] * c_col[h:h + 1, :]
                                                        + st[r0:r0 + SSM_HEAD_DIM, :])


PROMPT_BLOCK_WEIGHTS = ["norm_mix", "w_z", "w_xbc", "w_dt", "w_u", "w_v", "gm_ln_g", "gm_ln_b",
                        "conv_w", "conv_b", "dt_bias", "a_log", "d_skip", "ssm_norm", "gm_ws", "gm_bs_t",
                        "expand_ch", "expand_col", "w_gate", "b_gate", "w_a_proj", "w_b_proj", "w_out",
                        "norm_mem", "w_mem_q", "w_mem_o"]


def _prompt_block_body(x_ref, kb_ref, vb_ref,
                       gmix_ref, wz_ref, wx_ref, wdt_ref, wu_ref, wv_ref, lng_ref, lnb_ref,
                       cw_ref, cb_ref, dtb_ref, alog_ref, dsk_ref, nw_ref, ws_ref, bst_ref, ex_ref, ecol_ref,
                       wg_ref, bg_ref, wa_ref, wb_ref, wo_ref, gmem_ref, wq_ref, wmo_ref,
                       x2_ref, hout_ref, cout_ref,
                       xp_scr, y_scr, acst_scr, yn_scr, sg_scr, o_scr):
    lc = PROMPT_CHUNK
    rows = x_ref.shape[0]
    c = pl.program_id(1)

    @pl.when(c == 0)
    def _():
        xp_scr[CONV_BASE - CONV_HIST:CONV_BASE, :] = jnp.zeros((CONV_HIST, SSM_CONV_DIM), F32)
        hout_ref[...] = jnp.zeros(hout_ref.shape, F32)

    x = x_ref[...]
    xn = _rms(x, gmix_ref[...]).astype(BF16)
    xbc = jnp.dot(xn, wx_ref[...], preferred_element_type=F32)
    cout_ref[...] = xbc[rows - CONV_HIST:rows, :]
    xp_scr[CONV_BASE:CONV_BASE + rows, :] = _bf16_round(xbc)
    z = jnp.dot(xn, wz_ref[...], preferred_element_type=F32)
    dt_raw = jnp.dot(xn, wdt_ref[...], preferred_element_type=F32)
    u, v = _gm_uv(xn, wu_ref, wv_ref, lng_ref, lnb_ref)

    tril = _tril(lc)
    for sub in range(rows // lc):
        r0 = sub * lc
        xc = _conv_silu(xp_scr, lc, cw_ref, cb_ref, CONV_BASE + r0)
        dt, acs = _decay_cumsum(dt_raw[r0:r0 + lc], dtb_ref, alog_ref, tril, lc)
        _ssd_wide(xc, dt, acs, tril, dsk_ref, ex_ref, ecol_ref, hout_ref, y_scr.at[pl.ds(r0, lc)], acst_scr.at[sub])
        _spatial_gate(u[r0:r0 + lc], v[r0:r0 + lc], tril, ws_ref, bst_ref, sg_scr.at[pl.ds(r0, lc)])
    xp_scr[CONV_BASE - CONV_HIST:CONV_BASE, :] = xp_scr[CONV_BASE + rows - CONV_HIST:CONV_BASE + rows, :]

    _gate_norm(y_scr[...], z, nw_ref, yn_scr)
    x1 = _gated_merge(x, xn, yn_scr[...], sg_scr[...], wg_ref, bg_ref, wa_ref, wb_ref, wo_ref)

    q = _bdot(_rms(x1, gmem_ref[...]), wq_ref[...]).astype(BF16)
    scale = MEM_HEAD_DIM ** -0.5
    for h in range(MEM_HEADS):
        lo = h * MEM_HEAD_DIM
        s = lax.dot_general(q[:, lo:lo + MEM_HEAD_DIM], kb_ref[h], NT_DIMS, preferred_element_type=F32) * scale
        o_scr[:, lo:lo + MEM_HEAD_DIM] = jnp.dot(_softmax_rows(s).astype(BF16), vb_ref[h],
                                                 preferred_element_type=F32).astype(BF16)
    x2_ref[...] = x1 + _bdot(o_scr[...], wmo_ref[...])


def _mixer_weight_specs(w, names):
    return [_full(w[n].shape) for n in names], [w[n] for n in names]


def _prompt_block(x, kb, vb, w, bn, seq):
    lc = PROMPT_ROWS if seq % PROMPT_ROWS == 0 else PROMPT_CHUNK
    hp = SSM_HEADS * SSM_HEAD_DIM
    m = kb.shape[2]
    blk = pl.BlockSpec((None, lc, D_MODEL), lambda b, c: (b, c, 0))
    kv_spec = pl.BlockSpec((None, MEM_HEADS, m, MEM_HEAD_DIM), lambda b, c: (b, 0, 0, 0))
    once = lambda shape: pl.BlockSpec(shape, lambda b, c: (0,) * len(shape), pipeline_mode=pl.Buffered(1))
    x2, h_new, conv_new = pl.pallas_call(
        _prompt_block_body,
        grid=(bn, seq // lc),
        in_specs=[blk, kv_spec, kv_spec] + [once(w[n].shape) for n in PROMPT_BLOCK_WEIGHTS],
        out_specs=[blk,
                   pl.BlockSpec((None, hp, SSM_STATE), lambda b, c: (b, 0, 0)),
                   pl.BlockSpec((None, CONV_HIST, SSM_CONV_DIM), lambda b, c: (b, 0, 0))],
        out_shape=[jax.ShapeDtypeStruct((bn, seq, D_MODEL), F32),
                   jax.ShapeDtypeStruct((bn, hp, SSM_STATE), F32),
                   jax.ShapeDtypeStruct((bn, CONV_HIST, SSM_CONV_DIM), F32)],
        scratch_shapes=[pltpu.VMEM((CONV_BASE + lc, SSM_CONV_DIM), F32),
                        pltpu.VMEM((lc, SSM_D_INNER), F32),
                        pltpu.VMEM((lc // PROMPT_CHUNK, LANES, PROMPT_CHUNK), F32),
                        pltpu.VMEM((lc, SSM_D_INNER), BF16),
                        pltpu.VMEM((lc, GM_WIDTH), BF16),
                        pltpu.VMEM((lc, D_MODEL), BF16)],
        compiler_params=pltpu.CompilerParams(dimension_semantics=("arbitrary", "arbitrary"),
                                             vmem_limit_bytes=PROMPT_BLOCK_VMEM_LIMIT),
        name="prompt_block",
    )(x.reshape(bn, seq, D_MODEL), kb, vb, *[w[n] for n in PROMPT_BLOCK_WEIGHTS])
    return x2.reshape(bn * seq, D_MODEL), h_new, conv_new


def _ssd_short(xc, dt, acs, lv, dsk_ref, ex_ref, h_ref, hout_ref):
    lc = xc.shape[0]
    tot = acs[lc - 1:lc, :]
    e_acs = jnp.exp(acs)
    dt_end = dt * jnp.exp(tot - acs)
    c_col = jnp.exp(acs.T[:, lc - 1:lc])
    row = lax.broadcasted_iota(jnp.int32, (lc, LANES), 0)
    head = lax.broadcasted_iota(jnp.int32, (lc, LANES), 1)
    bms, cms, cbs = [], [], []
    for g in range(SSM_GROUPS):
        b0 = SSM_D_INNER + g * SSM_STATE
        c0 = SSM_D_INNER + SSM_GROUPS * SSM_STATE + g * SSM_STATE
        bms.append(xc[:, b0:b0 + SSM_STATE].astype(BF16))
        cms.append(xc[:, c0:c0 + SSM_STATE].astype(BF16))
        cbs.append(lax.dot_general(cms[g], bms[g], NT_DIMS, preferred_element_type=F32))
    coef = []
    for s in range(lv):
        decay = jnp.where(row >= s, jnp.exp(acs - acs[s:s + 1, :]), 0.0) * dt[s:s + 1, :]
        cb_s = jnp.zeros((lc, LANES), F32)
        for g in range(SSM_GROUPS):
            in_group = (head >= g * SSM_HEADS_PER_GROUP) & (head < (g + 1) * SSM_HEADS_PER_GROUP)
            cb_s = jnp.where(in_group, cbs[g][:, s:s + 1], cb_s)
        coef.append(decay * cb_s)
    wide = jnp.dot(_split_bf16(jnp.concatenate(coef + [e_acs, dt_end], axis=0)), ex_ref[...],
                   preferred_element_type=F32)
    xs = xc[:, :SSM_D_INNER]
    y = dsk_ref[...] * xs
    for s in range(lv):
        y = y + wide[s * lc:(s + 1) * lc] * xs[s:s + 1, :]
    e_acsx = wide[lv * lc:(lv + 1) * lc]
    xdte_b = (xs * wide[(lv + 1) * lc:]).astype(BF16)
    y_off = []
    for g in range(SSM_GROUPS):
        g0 = g * SSM_GROUP_WIDTH
        h_grp = h_ref[g0:g0 + SSM_GROUP_WIDTH, :]
        y_off.append(lax.dot_general(cms[g], h_grp.astype(BF16), NT_DIMS, preferred_element_type=F32))
        st = lax.dot_general(xdte_b[:, g0:g0 + SSM_GROUP_WIDTH], bms[g], TN_DIMS, preferred_element_type=F32)
        for r in range(SSM_HEADS_PER_GROUP):
            h = g * SSM_HEADS_PER_GROUP + r
            r0 = r * SSM_HEAD_DIM
            hout_ref[g0 + r0:g0 + r0 + SSM_HEAD_DIM, :] = (h_grp[r0:r0 + SSM_HEAD_DIM, :] * c_col[h:h + 1, :]
                                                           + st[r0:r0 + SSM_HEAD_DIM, :])
    return y + jnp.concatenate(y_off, axis=1) * e_acsx


def _mixer_sample_body(*refs, lv, sb, stack_prev):
    xbc_ref, dt_ref, z_ref, u_ref, v_ref, cprev_ref, h0_ref = refs[:7]
    i = 7
    if stack_prev:
        hprev_ref = refs[i]
        i += 1
    cw_ref, cb_ref, dtb_ref, alog_ref, dsk_ref, nw_ref, ws_ref, bst_ref, ex_ref = refs[i:i + 9]
    y_ref, sg_ref, hout_ref, cout_ref, xp_scr = refs[i + 9:]
    lc = SAMPLE_ROWS
    tril = _tril(lc)
    for s in range(sb):
        win = xp_scr.at[s]
        xbc = xbc_ref[s]
        win[CONV_BASE - CONV_HIST:CONV_BASE, :] = _bf16_round(cprev_ref[s])
        win[CONV_BASE:CONV_BASE + lc, :] = _bf16_round(xbc)
        xc = _conv_silu(win, lc, cw_ref, cb_ref)
        cout_ref[s] = xbc[lv - CONV_HIST:lv, :]
        dt, acs = _decay_cumsum(dt_ref[s], dtb_ref, alog_ref, tril, lv)
        if stack_prev:
            hout_ref[0, s] = hprev_ref[s]
            h_out = hout_ref.at[1, s]
        else:
            h_out = hout_ref.at[s]
        y = _ssd_short(xc, dt, acs, lv, dsk_ref, ex_ref, h0_ref.at[s], h_out)
        _gate_norm(y, z_ref[s], nw_ref, y_ref.at[s])
        _spatial_gate(u_ref[s], v_ref[s], tril, ws_ref, bst_ref, sg_ref.at[s])


def _mixer_sample(z, xbc, dt, u, v, w, bn, lv, state_ssm, state_conv, layer, h_prev_layer):
    lc = SAMPLE_ROWS
    sb = SAMPLE_SEQS_PER_STEP if bn % SAMPLE_SEQS_PER_STEP == 0 else 1
    stack_prev = h_prev_layer is not None
    hp = SSM_HEADS * SSM_HEAD_DIM
    r3 = lambda a: a.reshape(bn, lc, a.shape[-1])
    blk = lambda n: pl.BlockSpec((sb, lc, n), lambda b: (b, 0, 0))
    in_specs = [blk(SSM_CONV_DIM), blk(LANES), blk(SSM_D_INNER), blk(GM_WIDTH), blk(GM_WIDTH),
                pl.BlockSpec((None, sb, CONV_HIST, SSM_CONV_DIM), lambda b: (layer, b, 0, 0)),
                pl.BlockSpec((None, sb, hp, SSM_STATE), lambda b: (layer, b, 0, 0))]
    args = [r3(xbc), r3(dt), r3(z), r3(u), r3(v), state_conv, state_ssm]
    if stack_prev:
        in_specs.append(pl.BlockSpec((sb, hp, SSM_STATE), lambda b: (b, 0, 0)))
        args.append(h_prev_layer)
        h_spec = pl.BlockSpec((2, sb, hp, SSM_STATE), lambda b: (0, b, 0, 0))
        h_shape = jax.ShapeDtypeStruct((2, bn, hp, SSM_STATE), F32)
    else:
        h_spec = pl.BlockSpec((sb, hp, SSM_STATE), lambda b: (b, 0, 0))
        h_shape = jax.ShapeDtypeStruct((bn, hp, SSM_STATE), F32)
    w_specs, w_args = _mixer_weight_specs(w, ["conv_w", "conv_b", "dt_bias", "a_log", "d_skip", "ssm_norm", "gm_ws",
                                              "gm_bs_t", "expand_ch"])
    y, sg, h_new, conv_new = pl.pallas_call(
        functools.partial(_mixer_sample_body, lv=lv, sb=sb, stack_prev=stack_prev),
        grid=(bn // sb,),
        in_specs=in_specs + w_specs,
        out_specs=[blk(SSM_D_INNER), blk(GM_WIDTH), h_spec,
                   pl.BlockSpec((sb, CONV_HIST, SSM_CONV_DIM), lambda b: (b, 0, 0))],
        out_shape=[jax.ShapeDtypeStruct((bn, lc, SSM_D_INNER), F32), jax.ShapeDtypeStruct((bn, lc, GM_WIDTH), F32),
                   h_shape, jax.ShapeDtypeStruct((bn, CONV_HIST, SSM_CONV_DIM), F32)],
        scratch_shapes=[pltpu.VMEM((sb, CONV_BASE + lc, SSM_CONV_DIM), F32)],
        compiler_params=_params(1),
        name="mixer_sample",
    )(*args, *w_args)
    t = bn * lc
    return y.reshape(t, SSM_D_INNER), sg.reshape(t, GM_WIDTH), h_new, conv_new


def _gated_merge(x, xn, y, sg, wg_ref, bg_ref, wa_ref, wb_ref, wo_ref):
    gates = jax.nn.sigmoid(jnp.dot(xn, wg_ref[...], preferred_element_type=F32) + bg_ref[...])
    merged = gates[:, :D_MODEL] * _bdot(y, wa_ref[...]) + gates[:, D_MODEL:] * _bdot(sg, wb_ref[...])
    return x + _bdot(merged, wo_ref[...])


def _merge_body(x_ref, y_ref, sg_ref, gmix_ref, wg_ref, bg_ref, wa_ref, wb_ref, wo_ref, gmem_ref, wq_ref,
                x1_ref, q_ref):
    x = x_ref[...]
    xn = _rms(x, gmix_ref[...]).astype(BF16)
    x1 = _gated_merge(x, xn, y_ref[...], sg_ref[...], wg_ref, bg_ref, wa_ref, wb_ref, wo_ref)
    x1_ref[...] = x1
    q_ref[...] = _bdot(_rms(x1, gmem_ref[...]), wq_ref[...]).astype(BF16)


def _merge(x, y, sg, w, tm):
    t = x.shape[0]
    row = lambda n: pl.BlockSpec((tm, n), lambda i: (i, 0))
    return pl.pallas_call(
        _merge_body,
        grid=(t // tm,),
        in_specs=[row(D_MODEL), row(SSM_D_INNER), row(GM_WIDTH), _full((1, D_MODEL)),
                  _full((D_MODEL, 2 * D_MODEL)), _full((1, 2 * D_MODEL)), _full((SSM_D_INNER, D_MODEL)),
                  _full((GM_WIDTH, D_MODEL)), _full((D_MODEL, D_MODEL)), _full((1, D_MODEL)), _full((D_MODEL, D_MODEL))],
        out_specs=[row(D_MODEL), row(D_MODEL)],
        out_shape=[jax.ShapeDtypeStruct((t, D_MODEL), F32), jax.ShapeDtypeStruct((t, D_MODEL), BF16)],
        compiler_params=_params(1),
        name="merge",
    )(x, y, sg, w["norm_mix"], w["w_gate"], w["b_gate"], w["w_a_proj"], w["w_b_proj"], w["w_out"],
      w["norm_mem"], w["w_mem_q"])


def _memkv_body(m_ref, g_ref, wk_ref, wv_ref, k_ref, v_ref, kb_ref, vb_ref):
    mn = _rms(m_ref[...], g_ref[...]).astype(BF16)
    k = jnp.dot(mn, wk_ref[...], preferred_element_type=F32)
    v = jnp.dot(mn, wv_ref[...], preferred_element_type=F32)
    for h in range(MEM_HEADS):
        lo = h * MEM_HEAD_DIM
        k_ref[:, h, :] = k[:, lo:lo + MEM_HEAD_DIM]
        v_ref[:, h, :] = v[:, lo:lo + MEM_HEAD_DIM]
        kb_ref[h] = k[:, lo:lo + MEM_HEAD_DIM].astype(BF16)
        vb_ref[h] = v[:, lo:lo + MEM_HEAD_DIM].astype(BF16)


def _memkv(mem, w):
    bn, m, _ = mem.shape
    kv_spec = pl.BlockSpec((None, m, MEM_HEADS, MEM_HEAD_DIM), lambda b: (b, 0, 0, 0))
    kv_shape = jax.ShapeDtypeStruct((bn, m, MEM_HEADS, MEM_HEAD_DIM), F32)
    hb_spec = pl.BlockSpec((None, MEM_HEADS, m, MEM_HEAD_DIM), lambda b: (b, 0, 0, 0))
    hb_shape = jax.ShapeDtypeStruct((bn, MEM_HEADS, m, MEM_HEAD_DIM), BF16)
    return pl.pallas_call(
        _memkv_body,
        grid=(bn,),
        in_specs=[pl.BlockSpec((None, m, D_MODEL), lambda b: (b, 0, 0)), _full((1, D_MODEL)),
                  _full((D_MODEL, D_MODEL)), _full((D_MODEL, D_MODEL))],
        out_specs=[kv_spec, kv_spec, hb_spec, hb_spec],
        out_shape=[kv_shape, kv_shape, hb_shape, hb_shape],
        compiler_params=_params(1),
        name="memkv",
    )(mem, w["norm_memkv"], w["w_mem_k"], w["w_mem_v"])


def _softmax_rows(s):
    e = jnp.exp(s - jnp.max(s, axis=-1, keepdims=True))
    return e / jnp.sum(e, axis=-1, keepdims=True)


def _attn_sample_body(q_ref, k_ref, v_ref, x_ref, wo_ref, out_ref, o_scr):
    sb, rows, _ = q_ref.shape
    m = k_ref.shape[1]
    shape = (MEM_HEADS * rows, m * MEM_HEADS)
    own_head = (lax.broadcasted_iota(jnp.int32, shape, 0) // rows
                == lax.broadcasted_iota(jnp.int32, shape, 1) % MEM_HEADS)
    for i in range(sb):
        k2 = k_ref[i].reshape(m * MEM_HEADS, MEM_HEAD_DIM).astype(BF16)
        v2 = v_ref[i].reshape(m * MEM_HEADS, MEM_HEAD_DIM).astype(BF16)
        q = q_ref[i].astype(F32)
        q4 = jnp.concatenate([q[:, h * MEM_HEAD_DIM:(h + 1) * MEM_HEAD_DIM] for h in range(MEM_HEADS)], axis=0)
        s = lax.dot_general(q4.astype(BF16), k2, NT_DIMS, preferred_element_type=F32) * (MEM_HEAD_DIM ** -0.5)
        p = _softmax_rows(jnp.where(own_head, s, -jnp.inf))
        o4 = jnp.dot(p.astype(BF16), v2, preferred_element_type=F32)
        for h in range(MEM_HEADS):
            o_scr[i * rows:(i + 1) * rows, h * MEM_HEAD_DIM:(h + 1) * MEM_HEAD_DIM] = o4[h * rows:(h + 1) * rows, :]
    proj = _bdot(o_scr[...], wo_ref[...])
    for i in range(sb):
        out_ref[i] = x_ref[i] + proj[i * rows:(i + 1) * rows, :]


def _attn_sample(q, k5, v5, layer, x1, w, bn, rows):
    m = k5.shape[2]
    sb = ATTN_SEQS_PER_STEP if bn % ATTN_SEQS_PER_STEP == 0 else 1
    r3 = lambda a: a.reshape(bn, rows, D_MODEL)
    blk = pl.BlockSpec((sb, rows, D_MODEL), lambda b: (b, 0, 0))
    kv_spec = pl.BlockSpec((None, sb, m, MEM_HEADS, MEM_HEAD_DIM), lambda b: (layer, b, 0, 0, 0))
    out = pl.pallas_call(
        _attn_sample_body,
        grid=(bn // sb,),
        in_specs=[blk, kv_spec, kv_spec, blk, _full((D_MODEL, D_MODEL))],
        out_specs=blk,
        out_shape=jax.ShapeDtypeStruct((bn, rows, D_MODEL), F32),
        scratch_shapes=[pltpu.VMEM((sb * rows, D_MODEL), F32)],
        compiler_params=_params(1),
        name="attn_sample",
    )(r3(q), k5, v5, r3(x1), w["w_mem_o"])
    return out.reshape(bn * rows, D_MODEL)


PACKED_WIDTH = D_MODEL // 2
HIGH_HALF = 0xFFFF0000


def _pack_bf16_pairs(x):
    bits = lax.bitcast_convert_type(_bf16_round(x), jnp.uint32)
    return (bits[:, :PACKED_WIDTH] >> 16) | (bits[:, PACKED_WIDTH:] & jnp.uint32(HIGH_HALF))


def _unpack_bf16_pairs(p):
    lo = lax.bitcast_convert_type(p << 16, F32)
    hi = lax.bitcast_convert_type(p & jnp.uint32(HIGH_HALF), F32)
    return jnp.concatenate([lo, hi], axis=1).astype(BF16)


def _first_max(vals):
    m = functools.reduce(jnp.maximum, vals)
    idx = jnp.full(m.shape, len(vals) - 1, jnp.int32)
    for j in range(len(vals) - 2, -1, -1):
        idx = jnp.where(vals[j] == m, j, idx)
    return m, idx


def _route_body(x_ref, g_ref, wr_ref, br_ref, xn_ref, eid_ref, rank_ref, wt_ref, cnt_ref, carry_scr, *, tm):
    @pl.when(pl.program_id(0) == 0)
    def _():
        carry_scr[...] = jnp.zeros(carry_scr.shape, F32)

    xn = _rms(x_ref[...], g_ref[...])
    xn_ref[...] = _pack_bf16_pairs(xn)
    lg = lax.dot_general(wr_ref[...].astype(BF16), xn.astype(BF16), NT_DIMS, preferred_element_type=F32) + br_ref[...]
    grp = [lg[j:j + 1, :] for j in range(MOE_GROUPS)]
    gmax, gid = _first_max(grp)
    gw = 1.0 / functools.reduce(jnp.add, [jnp.exp(r - gmax) for r in grp])
    ex = [lg[SUBLANES + j:SUBLANES + j + 1, :] for j in range(MOE_EXPERTS)]
    sel = []
    for j in range(MOE_EXPERTS_PER_GROUP):
        pick = ex[(MOE_GROUPS - 1) * MOE_EXPERTS_PER_GROUP + j]
        for g in range(MOE_GROUPS - 2, -1, -1):
            pick = jnp.where(gid == g, ex[g * MOE_EXPERTS_PER_GROUP + j], pick)
        sel.append(pick)
    v1, i1 = _first_max(sel)
    rest = [jnp.where(i1 == j, -jnp.inf, sel[j]) for j in range(MOE_EXPERTS_PER_GROUP)]
    v2, i2 = _first_max(rest)
    e2 = jnp.exp(v2 - v1)
    den = 1.0 + e2
    eid1 = gid * MOE_EXPERTS_PER_GROUP + i1
    eid2 = gid * MOE_EXPERTS_PER_GROUP + i2

    e_iota = lax.broadcasted_iota(jnp.int32, (MOE_EXPERTS, tm), 0)
    m1 = e_iota == eid1
    m2 = e_iota == eid2
    onehot = jnp.where(m1, 1.0, 0.0) + jnp.where(m2, 1.0, 0.0)
    before = lax.broadcasted_iota(jnp.int32, (tm, tm), 0) < lax.broadcasted_iota(jnp.int32, (tm, tm), 1)
    ranks = _bdot(onehot, jnp.where(before, 1.0, 0.0)) + carry_scr[:, 0:1]
    r1 = jnp.sum(jnp.where(m1, ranks, 0.0), axis=0, keepdims=True)
    r2 = jnp.sum(jnp.where(m2, ranks, 0.0), axis=0, keepdims=True)
    carry_scr[...] = carry_scr[...] + jnp.sum(onehot, axis=1, keepdims=True)
    cnt_ref[...] = carry_scr[...]

    zeros_i = jnp.zeros((SUBLANES - MOE_TOP_K, tm), jnp.int32)
    eid_ref[0:1, :] = eid1
    eid_ref[1:2, :] = eid2
    eid_ref[MOE_TOP_K:, :] = zeros_i
    rank_ref[0:1, :] = r1.astype(jnp.int32)
    rank_ref[1:2, :] = r2.astype(jnp.int32)
    rank_ref[MOE_TOP_K:, :] = zeros_i
    wt_ref[0:1, :] = gw / den
    wt_ref[1:2, :] = gw * e2 / den
    wt_ref[MOE_TOP_K:, :] = jnp.zeros((SUBLANES - MOE_TOP_K, tm), F32)


def _route(x, w, tm):
    t = x.shape[0]
    lane_blk = pl.BlockSpec((SUBLANES, tm), lambda i: (0, i))
    nr = w["w_router_t"].shape[0]
    return pl.pallas_call(
        functools.partial(_route_body, tm=tm),
        grid=(t // tm,),
        in_specs=[pl.BlockSpec((tm, D_MODEL), lambda i: (i, 0)), _full((1, D_MODEL)), _full((nr, D_MODEL)), _full((nr, 1))],
        out_specs=[pl.BlockSpec((tm, PACKED_WIDTH), lambda i: (i, 0)), lane_blk, lane_blk, lane_blk,
                   _full((MOE_EXPERTS, LANES))],
        out_shape=[jax.ShapeDtypeStruct((t, PACKED_WIDTH), jnp.uint32), jax.ShapeDtypeStruct((SUBLANES, t), jnp.int32),
                   jax.ShapeDtypeStruct((SUBLANES, t), jnp.int32), jax.ShapeDtypeStruct((SUBLANES, t), F32),
                   jax.ShapeDtypeStruct((MOE_EXPERTS, LANES), F32)],
        scratch_shapes=[pltpu.VMEM((MOE_EXPERTS, LANES), F32)],
        compiler_params=_params(1),
        name="route",
    )(x, w["norm_ffn"], w["w_router_t"], w["b_router_t"])


def _row_copy(src, src_row, dst, dst_row, sem):
    return pltpu.make_async_copy(src.at[pl.ds(src_row, 1), :], dst.at[pl.ds(dst_row, 1), :], sem)


META_ENDS = 0
META_SIZES = MOE_EXPERTS
META_USED = 2 * MOE_EXPERTS


def _dispatch_body(meta_ref, dest_ref, prev_dest_ref, xn_ref, out_ref, zbuf, stage, sem, zsem, *, tm, rows, n_blocks):
    i = pl.program_id(0)
    slot = i % 2

    @pl.when(i == 0)
    def _():
        zbuf[...] = jnp.zeros(zbuf.shape, zbuf.dtype)

        def zero_fills(act):
            for e in range(MOE_EXPERTS):
                @pl.when(meta_ref[META_SIZES + e] > 0)
                def _(e=e):
                    lo = pl.multiple_of(meta_ref[META_ENDS + e] - rows, rows)
                    act(pltpu.make_async_copy(zbuf, out_ref.at[pl.ds(lo, rows), :], zsem))
            for j in range(MOE_EXPERTS):
                @pl.when(meta_ref[META_USED] + j < n_blocks)
                def _(j=j):
                    lo = pl.multiple_of((meta_ref[META_USED] + j) * rows, rows)
                    act(pltpu.make_async_copy(zbuf, out_ref.at[pl.ds(lo, rows), :], zsem))

        zero_fills(lambda cp: cp.start())
        zero_fills(lambda cp: cp.wait())

    def copies(idx_ref, s, act):
        def body(t, carry):
            for k in range(MOE_TOP_K):
                act(_row_copy(stage.at[s], t, out_ref, idx_ref[k, t], sem.at[s]))
            return carry
        lax.fori_loop(0, tm, body, 0, unroll=DMA_UNROLL)

    stage[slot] = xn_ref[...]
    copies(dest_ref, slot, lambda cp: cp.start())

    @pl.when(i > 0)
    def _():
        copies(prev_dest_ref, 1 - slot, lambda cp: cp.wait())

    @pl.when(i == pl.num_programs(0) - 1)
    def _():
        copies(dest_ref, slot, lambda cp: cp.wait())


def _dispatch(xn, dest, meta, n_blocks, rows, tm):
    t = xn.shape[0]
    grid_spec = pltpu.PrefetchScalarGridSpec(
        num_scalar_prefetch=1,
        grid=(t // tm,),
        in_specs=[pl.BlockSpec((MOE_TOP_K, tm), lambda i, meta: (0, i), memory_space=pltpu.SMEM),
                  pl.BlockSpec((MOE_TOP_K, tm), lambda i, meta: (0, jnp.maximum(i - 1, 0)), memory_space=pltpu.SMEM),
                  pl.BlockSpec((tm, PACKED_WIDTH), lambda i, meta: (i, 0))],
        out_specs=pl.BlockSpec(memory_space=pl.ANY),
        scratch_shapes=[pltpu.VMEM((rows, PACKED_WIDTH), jnp.uint32), pltpu.VMEM((2, tm, PACKED_WIDTH), jnp.uint32),
                        pltpu.SemaphoreType.DMA((2,)), pltpu.SemaphoreType.DMA(())],
    )
    return pl.pallas_call(
        functools.partial(_dispatch_body, tm=tm, rows=rows, n_blocks=n_blocks),
        grid_spec=grid_spec,
        out_shape=jax.ShapeDtypeStruct((n_blocks * rows, PACKED_WIDTH), jnp.uint32),
        compiler_params=_params(1, disable_bounds_checks=True),
        name="dispatch",
    )(meta, dest, dest, xn)


def _ffn_body(be_ref, meta_ref, x_ref, wg_ref, wu_ref, wd_ref, o_ref, wg_scr, wu_scr, wd_scr):
    i = pl.program_id(0)
    last = jnp.maximum(meta_ref[META_USED] - 1, 0)
    used = i < meta_ref[META_USED]
    e_now = be_ref[jnp.minimum(i, last)]
    e_before = be_ref[jnp.minimum(jnp.maximum(i - 1, 0), last)]

    @pl.when(jnp.logical_or(i == 0, e_now != e_before))
    def _():
        wg_scr[...] = wg_ref[...].astype(BF16)
        wu_scr[...] = wu_ref[...].astype(BF16)
        wd_scr[...] = wd_ref[...].astype(BF16)

    @pl.when(used)
    def _():
        x = _unpack_bf16_pairs(x_ref[...])
        hid = (_silu(jnp.dot(x, wg_scr[...], preferred_element_type=F32))
               * jnp.dot(x, wu_scr[...], preferred_element_type=F32))
        o_ref[...] = jnp.dot(hid.astype(BF16), wd_scr[...], preferred_element_type=F32)

    @pl.when(jnp.logical_not(used))
    def _():
        o_ref[...] = jnp.zeros(o_ref.shape, F32)


def _ffn(xs, block_e, meta, w, layer, rows):
    n_slots = xs.shape[0]
    last_used = lambda i, meta: jnp.minimum(i, jnp.maximum(meta[META_USED] - 1, 0))
    blk_i = lambda i, be, meta: (last_used(i, meta), 0)
    exp_i = lambda i, be, meta: (layer, be[last_used(i, meta)], 0, 0)
    grid_spec = pltpu.PrefetchScalarGridSpec(
        num_scalar_prefetch=2,
        grid=(n_slots // rows,),
        in_specs=[pl.BlockSpec((rows, PACKED_WIDTH), blk_i),
                  pl.BlockSpec((None, None, D_MODEL, MOE_D_FF), exp_i),
                  pl.BlockSpec((None, None, D_MODEL, MOE_D_FF), exp_i),
                  pl.BlockSpec((None, None, MOE_D_FF, D_MODEL), exp_i)],
        out_specs=pl.BlockSpec((rows, D_MODEL), lambda i, be, meta: (i, 0)),
        scratch_shapes=[pltpu.VMEM((D_MODEL, MOE_D_FF), BF16), pltpu.VMEM((D_MODEL, MOE_D_FF), BF16),
                        pltpu.VMEM((MOE_D_FF, D_MODEL), BF16)],
    )
    return pl.pallas_call(
        _ffn_body,
        grid_spec=grid_spec,
        out_shape=jax.ShapeDtypeStruct((n_slots, D_MODEL), F32),
        compiler_params=_params(1),
        name="ffn",
    )(block_e, meta, xs, w["w_expert_gate"], w["w_expert_up"], w["w_expert_down"])


def _combine_body(dest_ref, prev_dest_ref, wt_ref, x_ref, ys_ref, gfin_ref, out_ref, bufs, sem, *, tm, final):
    i = pl.program_id(0)
    n_tiles = pl.num_programs(0) - 1
    slot = i % 2

    def copies(idx_ref, s, act):
        def body(t, carry):
            for k in range(MOE_TOP_K):
                act(_row_copy(ys_ref, idx_ref[k, t], bufs.at[s, k], t, sem.at[s]))
            return carry
        lax.fori_loop(0, tm, body, 0, unroll=DMA_UNROLL)

    @pl.when(i < n_tiles)
    def _():
        copies(dest_ref, slot, lambda cp: cp.start())

    @pl.when(i > 0)
    def _():
        copies(prev_dest_ref, 1 - slot, lambda cp: cp.wait())
        wt = wt_ref[...].T
        out = x_ref[...] + (wt[:, 0:1] * bufs[1 - slot, 0] + wt[:, 1:2] * bufs[1 - slot, 1])
        if final:
            out = _rms(out, gfin_ref[...])
        out_ref[...] = out


def _combine(x, ys, dest, wt, g_final, tm, final):
    t = x.shape[0]
    n_tiles = t // tm
    cur = lambda i: jnp.minimum(i, n_tiles - 1)
    prev = lambda i: jnp.maximum(i - 1, 0)
    return pl.pallas_call(
        functools.partial(_combine_body, tm=tm, final=final),
        grid=(n_tiles + 1,),
        in_specs=[pl.BlockSpec((MOE_TOP_K, tm), lambda i: (0, cur(i)), memory_space=pltpu.SMEM),
                  pl.BlockSpec((MOE_TOP_K, tm), lambda i: (0, prev(i)), memory_space=pltpu.SMEM),
                  pl.BlockSpec((SUBLANES, tm), lambda i: (0, prev(i))),
                  pl.BlockSpec((tm, D_MODEL), lambda i: (prev(i), 0)),
                  pl.BlockSpec(memory_space=pl.ANY),
                  _full((1, D_MODEL))],
        out_specs=pl.BlockSpec((tm, D_MODEL), lambda i: (prev(i), 0)),
        out_shape=jax.ShapeDtypeStruct((t, D_MODEL), F32),
        scratch_shapes=[pltpu.VMEM((2, MOE_TOP_K, tm, D_MODEL), F32), pltpu.SemaphoreType.DMA((2,))],
        compiler_params=_params(1, disable_bounds_checks=True),
        name="combine",
    )(dest, dest, wt, x, ys, g_final)


def _moe(x, w, layer, g_final, final, tm, rows):
    t = x.shape[0]
    xn, eid, rank, wt, cnt = _route(x, w, tm)
    counts = cnt[:, 0].astype(jnp.int32)
    padded = (counts + rows - 1) // rows * rows
    pad_ends = jnp.cumsum(padded)
    pad_starts = pad_ends - padded
    n_blocks = (t * MOE_TOP_K) // rows + MOE_EXPERTS
    e_ids = jnp.arange(MOE_EXPERTS, dtype=jnp.int32)
    start_of = jnp.sum(jnp.where(eid[:MOE_TOP_K, :, None] == e_ids, pad_starts, 0), axis=-1)
    dest = start_of + rank[:MOE_TOP_K]
    block_lo = jnp.arange(n_blocks, dtype=jnp.int32) * rows
    block_e = jnp.minimum(jnp.sum((pad_ends[None, :] <= block_lo[:, None]).astype(jnp.int32), axis=1), MOE_EXPERTS - 1)
    meta = jnp.concatenate([pad_ends, padded, pad_ends[-1:] // rows]).astype(jnp.int32)
    xs = _dispatch(xn, dest, meta, n_blocks, rows, tm)
    ys = _ffn(xs, block_e, meta, w, layer, rows)
    return _combine(x, ys, dest, wt, g_final, tm, final)


def _expansion_tables():
    k = jnp.arange(SPLIT_PARTS * LANES, dtype=jnp.int32)[:, None] % LANES
    ch = jnp.arange(SSM_D_INNER, dtype=jnp.int32)[None, :] // SSM_HEAD_DIM
    col = jnp.arange(SSM_HEADS * LANES, dtype=jnp.int32)[None, :] // LANES
    return (k == ch).astype(BF16), (k == col).astype(BF16)


def _layer_weights(l, p):
    row = lambda a: a[l].reshape(1, -1)
    s1 = SSM_D_INNER
    s2 = s1 + SSM_CONV_DIM
    s3 = s2 + SSM_HEADS
    w_in = p["w_in"][l]
    pad_h = LANES - SSM_HEADS
    wr = jnp.zeros((SUBLANES + MOE_EXPERTS, D_MODEL), F32)
    wr = wr.at[:MOE_GROUPS].set(p["w_router_group"][l].T).at[SUBLANES:].set(p["w_router_expert"][l].T)
    br = jnp.zeros((SUBLANES + MOE_EXPERTS, 1), F32)
    br = br.at[:MOE_GROUPS, 0].set(p["b_router_group"][l]).at[SUBLANES:, 0].set(p["b_router_expert"][l].reshape(-1))
    expand_ch, expand_col = _expansion_tables()
    return {
        "norm_mix": row(p["norm_mix"]),
        "w_z": w_in[:, :s1].astype(BF16),
        "w_xbc": w_in[:, s1:s2].astype(BF16),
        "w_dt": jnp.pad(w_in[:, s2:s3], ((0, 0), (0, pad_h))).astype(BF16),
        "w_u": w_in[:, s3:s3 + GM_WIDTH].astype(BF16),
        "w_v": w_in[:, s3 + GM_WIDTH:].astype(BF16),
        "gm_ln_g": row(p["gm_ln_g"]), "gm_ln_b": row(p["gm_ln_b"]),
        "conv_w": p["conv_w"][l], "conv_b": row(p["conv_b"]),
        "dt_bias": jnp.pad(row(p["dt_bias"]), ((0, 0), (0, pad_h))),
        "a_log": jnp.pad(row(p["a_log"]), ((0, 0), (0, pad_h))),
        "d_skip": jnp.repeat(p["d_skip"][l], SSM_HEAD_DIM).reshape(1, -1),
        "ssm_norm": row(p["ssm_norm"]),
        "gm_ws": p["gm_ws"][l], "gm_bs_t": p["gm_bs"][l].T,
        "expand_ch": expand_ch, "expand_col": expand_col,
        "w_gate": p["w_gate"][l].astype(BF16), "b_gate": row(p["b_gate"]),
        "w_a_proj": p["w_a_proj"][l].astype(BF16), "w_b_proj": p["w_b_proj"][l].astype(BF16),
        "w_out": p["w_out"][l].astype(BF16),
        "norm_mem": row(p["norm_mem"]), "norm_memkv": row(p["norm_memkv"]),
        "w_mem_q": p["w_mem_q"][l].astype(BF16), "w_mem_k": p["w_mem_k"][l].astype(BF16),
        "w_mem_v": p["w_mem_v"][l].astype(BF16), "w_mem_o": p["w_mem_o"][l].astype(BF16),
        "norm_ffn": row(p["norm_ffn"]),
        "w_router_t": wr, "b_router_t": br,
        "w_expert_gate": p["w_expert_gate"], "w_expert_up": p["w_expert_up"], "w_expert_down": p["w_expert_down"],
    }


def kernel(x_prompt, x_sample, mem_prompt, state_ssm, state_conv, cache_mem_k, cache_mem_v, norm_mix, w_in, conv_w, conv_b, dt_bias, a_log, d_skip, ssm_norm, w_a_proj, gm_ln_g, gm_ln_b, gm_ws, gm_bs, w_b_proj, w_gate, b_gate, w_out, norm_mem, norm_memkv, w_mem_q, w_mem_k, w_mem_v, w_mem_o, norm_ffn, w_router_group, b_router_group, w_router_expert, b_router_expert, w_expert_gate, w_expert_up, w_expert_down, norm_final):
    p = dict(norm_mix=norm_mix, w_in=w_in, conv_w=conv_w, conv_b=conv_b, dt_bias=dt_bias, a_log=a_log, d_skip=d_skip,
             ssm_norm=ssm_norm, w_a_proj=w_a_proj, gm_ln_g=gm_ln_g, gm_ln_b=gm_ln_b, gm_ws=gm_ws, gm_bs=gm_bs,
             w_b_proj=w_b_proj, w_gate=w_gate, b_gate=b_gate, w_out=w_out, norm_mem=norm_mem, norm_memkv=norm_memkv,
             w_mem_q=w_mem_q, w_mem_k=w_mem_k, w_mem_v=w_mem_v, w_mem_o=w_mem_o, norm_ffn=norm_ffn,
             w_router_group=w_router_group, b_router_group=b_router_group, w_router_expert=w_router_expert,
             b_router_expert=b_router_expert, w_expert_gate=w_expert_gate, w_expert_up=w_expert_up,
             w_expert_down=w_expert_down)
    depth = w_in.shape[0]
    bp, lp, _ = x_prompt.shape
    bs, ls, _ = x_sample.shape
    assert lp % PROMPT_CHUNK == 0 and ls <= SAMPLE_ROWS
    g_final = norm_final.reshape(1, -1)
    hp = SSM_HEADS * SSM_HEAD_DIM
    state_shape = (SSM_HEADS, SSM_HEAD_DIM, SSM_STATE)
    state_ssm2 = state_ssm.reshape(depth, bs, hp, SSM_STATE)

    xp = x_prompt.reshape(bp * lp, D_MODEL)
    xs = jnp.pad(x_sample, ((0, 0), (0, SAMPLE_ROWS - ls), (0, 0))).reshape(bs * SAMPLE_ROWS, D_MODEL)
    tp, ts = bp * lp, bs * SAMPLE_ROWS
    tm_p, tm_s = min(256, tp), min(256, ts)
    tmm_p, tmm_s = min(512, tp), min(512, ts)
    lq_p = min(1024, lp)

    ssm_p, conv_p, mk_p, mv_p, ssm_s, conv_s, gv_s = [], [], [], [], [], [], []
    for l in range(depth):
        w = _layer_weights(l, p)
        final = l == depth - 1
        k_p, v_p, kb, vb = _memkv(mem_prompt, w)
        x2, h_new, c_new = _prompt_block(xp, kb, vb, w, bp, lp)
        xp = _moe(x2, w, l, g_final, final, tmm_p, MOE_ROWS_PROMPT)
        ssm_p.append(h_new.reshape((bp,) + state_shape))
        conv_p.append(c_new)
        mk_p.append(k_p)
        mv_p.append(v_p)
        z, xbc, dt, u, v = _inproj(xs, w, tm_s)
        stack_two = depth == 2 and l == 1
        y, sg, h_new, c_new = _mixer_sample(z, xbc, dt, u, v, w, bs, ls, state_ssm2, state_conv, l,
                                            ssm_s[0] if stack_two else None)
        x1, q = _merge(xs, y, sg, w, tm_s)
        x2 = _attn_sample(q, cache_mem_k, cache_mem_v, l, x1, w, bs, SAMPLE_ROWS)
        xs = _moe(x2, w, l, g_final, final, tmm_s, MOE_ROWS_SAMPLE)
        if stack_two:
            ssm_s = h_new.reshape((depth, bs) + state_shape)
        else:
            ssm_s.append(h_new)
        conv_s.append(c_new)
        gv_s.append(v.reshape(bs, SAMPLE_ROWS, GM_WIDTH)[:, :ls])
    if isinstance(ssm_s, list):
        ssm_s = jnp.stack(ssm_s).reshape((depth, bs) + state_shape)
    y_prompt = xp.reshape(bp, lp, D_MODEL)
    y_sample = xs.reshape(bs, SAMPLE_ROWS, D_MODEL)[:, :ls]
    return (y_prompt, y_sample, jnp.stack(ssm_p), jnp.stack(conv_p), jnp.stack(mk_p), jnp.stack(mv_p),
            ssm_s, jnp.stack(conv_s), jnp.stack(gv_s))
```

```python
import functools

import jax
import jax.numpy as jnp
from jax import lax
from jax.experimental import pallas as pl
from jax.experimental.pallas import tpu as pltpu

F32 = jnp.float32
BF16 = jnp.bfloat16
HIGHEST = lax.Precision.HIGHEST

NORM_EPS = 1e-6
D_MODEL = 1024
SSM_D_INNER = 1536
SSM_HEAD_DIM = 64
SSM_HEADS = 24
SSM_GROUPS = 4
SSM_HEADS_PER_GROUP = 6
SSM_STATE = 128
SSM_CONV = 4
SSM_CONV_DIM = 2560
SSM_GROUP_WIDTH = SSM_D_INNER // SSM_GROUPS
GM_WIDTH = 512
GM_GROUPS = 4
GM_GROUP_DIM = 128
MEM_HEADS = 4
MEM_HEAD_DIM = 256
MOE_GROUPS = 4
MOE_EXPERTS_PER_GROUP = 4
MOE_EXPERTS = 16
MOE_TOP_K = 2
MOE_D_FF = 512

LANES = 128
SUBLANES = 8
SAMPLE_ROWS = 8
SAMPLE_SEQS_PER_STEP = 4
ATTN_SEQS_PER_STEP = 4
PROMPT_CHUNK = 128
PROMPT_ROWS = 256
MOE_ROWS_PROMPT = 512
MOE_ROWS_SAMPLE = 128
SPLIT_PARTS = 3
DMA_UNROLL = 8
VMEM_LIMIT = 48 * 1024 * 1024
PROMPT_BLOCK_VMEM_LIMIT = 56 * 1024 * 1024

NT_DIMS = (((1,), (1,)), ((), ()))
TN_DIMS = (((0,), (0,)), ((), ()))


def _params(n_axes, **kw):
    return pltpu.CompilerParams(dimension_semantics=("arbitrary",) * n_axes, vmem_limit_bytes=VMEM_LIMIT, **kw)


def _rms(x, g):
    return x * lax.rsqrt(jnp.mean(x * x, axis=-1, keepdims=True) + NORM_EPS) * g


def _silu(x):
    return x * jax.nn.sigmoid(x)


def _gelu(x):
    return 0.5 * x * (1.0 + lax.erf(x * (2.0 ** -0.5)))


def _bdot(a, b):
    return jnp.dot(a.astype(BF16), b.astype(BF16), preferred_element_type=F32)


def _full(shape):
    n = len(shape)
    return pl.BlockSpec(shape, lambda *_: (0,) * n)


def _gm_uv(xn, wu_ref, wv_ref, lng_ref, lnb_ref):
    u = _gelu(jnp.dot(xn, wu_ref[...], preferred_element_type=F32))
    v = _gelu(jnp.dot(xn, wv_ref[...], preferred_element_type=F32))
    vc = v - jnp.mean(v, axis=-1, keepdims=True)
    var = jnp.mean(vc * vc, axis=-1, keepdims=True)
    return u, vc * lax.rsqrt(var + NORM_EPS) * lng_ref[...] + lnb_ref[...]


def _inproj_body(x_ref, g_ref, wz_ref, wx_ref, wdt_ref, wu_ref, wv_ref, lng_ref, lnb_ref,
                 z_ref, xbc_ref, dt_ref, u_ref, v_ref):
    xn = _rms(x_ref[...], g_ref[...]).astype(BF16)
    z_ref[...] = jnp.dot(xn, wz_ref[...], preferred_element_type=F32)
    xbc_ref[...] = jnp.dot(xn, wx_ref[...], preferred_element_type=F32)
    dt_ref[...] = jnp.dot(xn, wdt_ref[...], preferred_element_type=F32)
    u_ref[...], v_ref[...] = _gm_uv(xn, wu_ref, wv_ref, lng_ref, lnb_ref)


def _inproj(x, w, tm):
    t = x.shape[0]
    row = lambda n: pl.BlockSpec((tm, n), lambda i: (i, 0))
    outs = [SSM_D_INNER, SSM_CONV_DIM, LANES, GM_WIDTH, GM_WIDTH]
    return pl.pallas_call(
        _inproj_body,
        grid=(t // tm,),
        in_specs=[row(D_MODEL), _full((1, D_MODEL)), _full((D_MODEL, SSM_D_INNER)), _full((D_MODEL, SSM_CONV_DIM)),
                  _full((D_MODEL, LANES)), _full((D_MODEL, GM_WIDTH)), _full((D_MODEL, GM_WIDTH)),
                  _full((1, GM_WIDTH)), _full((1, GM_WIDTH))],
        out_specs=[row(n) for n in outs],
        out_shape=[jax.ShapeDtypeStruct((t, n), F32) for n in outs],
        compiler_params=_params(1),
        name="inproj",
    )(x, w["norm_mix"], w["w_z"], w["w_xbc"], w["w_dt"], w["w_u"], w["w_v"], w["gm_ln_g"], w["gm_ln_b"])


CONV_HIST = SSM_CONV - 1
CONV_BASE = SUBLANES


def _split_bf16(x):
    parts = []
    rest = x
    for _ in range(SPLIT_PARTS):
        piece = rest.astype(BF16)
        parts.append(piece)
        rest = rest - piece.astype(F32)
    return jnp.concatenate(parts, axis=1)


def _bf16_round(x):
    return x.astype(BF16).astype(F32)


def _conv_silu(win_ref, lc, cw_ref, cb_ref, base=CONV_BASE):
    xc = cb_ref[...]
    for k in range(SSM_CONV):
        lo = base - CONV_HIST + k
        xc = xc + _bf16_round(cw_ref[k:k + 1, :]) * win_ref[lo:lo + lc, :]
    return _silu(xc)


def _decay_cumsum(dt_raw, dtb_ref, alog_ref, tril, lv):
    lc = dt_raw.shape[0]
    dt = jax.nn.softplus(dt_raw + dtb_ref[...])
    if lv < lc:
        dt = jnp.where(lax.broadcasted_iota(jnp.int32, (lc, LANES), 0) < lv, dt, 0.0)
    a = -jnp.exp(alog_ref[...])
    acs = jnp.dot(tril.astype(F32), dt * a, precision=HIGHEST, preferred_element_type=F32)
    return dt, acs


def _gate_norm(y, z, nw_ref, out_ref):
    yz = y * _silu(z)
    for g in range(SSM_GROUPS):
        s0 = g * SSM_GROUP_WIDTH
        part = yz[:, s0:s0 + SSM_GROUP_WIDTH]
        ms = jnp.mean(part * part, axis=-1, keepdims=True)
        normed = part * lax.rsqrt(ms + NORM_EPS) * nw_ref[:, s0:s0 + SSM_GROUP_WIDTH]
        out_ref[:, s0:s0 + SSM_GROUP_WIDTH] = normed.astype(out_ref.dtype)


def _spatial_gate(u, v, tril, ws_ref, bst_ref, out_ref):
    lc = u.shape[0]
    for g in range(GM_GROUPS):
        k0 = g * GM_GROUP_DIM
        wc = jnp.where(tril, ws_ref[g, 0:lc, 0:lc], 0.0)
        s = _bdot(wc, v[:, k0:k0 + GM_GROUP_DIM]) + bst_ref[0:lc, g:g + 1]
        out_ref[:, k0:k0 + GM_GROUP_DIM] = (u[:, k0:k0 + GM_GROUP_DIM] * s).astype(out_ref.dtype)


def _tril(lc):
    return lax.broadcasted_iota(jnp.int32, (lc, lc), 0) >= lax.broadcasted_iota(jnp.int32, (lc, lc), 1)


def _ssd_wide(xc, dt, acs, tril, dsk_ref, ex_ref, ecol_ref, h_scr, y_scr, acst_scr):
    lc = xc.shape[0]
    acs_parts = _split_bf16(acs)
    wide = jnp.dot(jnp.concatenate([_split_bf16(dt), acs_parts], axis=0), ex_ref[...], preferred_element_type=F32)
    dtx = wide[:lc]
    acsx = wide[lc:]
    colb = jnp.dot(acs_parts, ecol_ref[...], preferred_element_type=F32)
    acst_scr[...] = acs.T
    c_col = jnp.exp(acst_scr[:, lc - 1:lc])
    xs = xc[:, :SSM_D_INNER]
    xdt = xs * dtx
    xdt_b = xdt.astype(BF16)
    xdte_b = (xdt * jnp.exp(acsx[lc - 1:lc, :] - acsx)).astype(BF16)
    e_acsx = jnp.exp(acsx)
    skip = dsk_ref[...] * xs
    first_head = lax.broadcasted_iota(jnp.int32, (lc, LANES), 1) < SSM_HEAD_DIM
    pair = 2 * SSM_HEAD_DIM
    for g in range(SSM_GROUPS):
        b0 = SSM_D_INNER + g * SSM_STATE
        c0 = SSM_D_INNER + SSM_GROUPS * SSM_STATE + g * SSM_STATE
        g0 = g * SSM_GROUP_WIDTH
        bm = xc[:, b0:b0 + SSM_STATE].astype(BF16)
        cm = xc[:, c0:c0 + SSM_STATE].astype(BF16)
        cb = lax.dot_general(cm, bm, NT_DIMS, preferred_element_type=F32)
        h_grp = h_scr[g0:g0 + SSM_GROUP_WIDTH, :]
        y_off = lax.dot_general(cm, h_grp.astype(BF16), NT_DIMS, preferred_element_type=F32)
        for j in range(SSM_HEADS_PER_GROUP // 2):
            h1 = g * SSM_HEADS_PER_GROUP + 2 * j
            p0 = h1 * SSM_HEAD_DIM
            atts = []
            for h in (h1, h1 + 1):
                seg = colb[:, h * LANES:(h + 1) * LANES] - acst_scr[h:h + 1, :]
                atts.append((cb * jnp.where(tril, jnp.exp(seg), 0.0)).astype(BF16))
            blk = xdt_b[:, p0:p0 + pair]
            zero = jnp.zeros_like(blk)
            rhs = jnp.concatenate([jnp.where(first_head, blk, zero), jnp.where(first_head, zero, blk)], axis=0)
            y_diag = jnp.dot(jnp.concatenate(atts, axis=1), rhs, preferred_element_type=F32)
            y_scr[:, p0:p0 + pair] = (y_diag + y_off[:, j * pair:(j + 1) * pair] * e_acsx[:, p0:p0 + pair]
                                      + skip[:, p0:p0 + pair])
        st = lax.dot_general(xdte_b[:, g0:g0 + SSM_GROUP_WIDTH], bm, TN_DIMS, preferred_element_type=F32)
        for r in range(SSM_HEADS_PER_GROUP):
            h = g * SSM_HEADS_PER_GROUP + r
            r0 = r * SSM_HEAD_DIM
            h_scr[g0 + r0:g0 + r0 + SSM_HEAD_DIM, :] = (h_grp[r0:r0 + SSM_HEAD_DIM, :] * c_col[h:h + 1, :]
                                                        + st[r0:r0 + SSM_HEAD_DIM, :])


PROMPT_BLOCK_WEIGHTS = ["norm_mix", "w_z", "w_xbc", "w_dt", "w_u", "w_v", "gm_ln_g", "gm_ln_b",
                        "conv_w", "conv_b", "dt_bias", "a_log", "d_skip", "ssm_norm", "gm_ws", "gm_bs_t",
                        "expand_ch", "expand_col", "w_gate", "b_gate", "w_a_proj", "w_b_proj", "w_out",
                        "norm_mem", "w_mem_q", "w_mem_o"]


def _prompt_block_body(x_ref, kb_ref, vb_ref,
                       gmix_ref, wz_ref, wx_ref, wdt_ref, wu_ref, wv_ref, lng_ref, lnb_ref,
                       cw_ref, cb_ref, dtb_ref, alog_ref, dsk_ref, nw_ref, ws_ref, bst_ref, ex_ref, ecol_ref,
                       wg_ref, bg_ref, wa_ref, wb_ref, wo_ref, gmem_ref, wq_ref, wmo_ref,
                       x2_ref, hout_ref, cout_ref,
                       xp_scr, y_scr, acst_scr, yn_scr, sg_scr, o_scr):
    lc = PROMPT_CHUNK
    rows = x_ref.shape[0]
    c = pl.program_id(1)

    @pl.when(c == 0)
    def _():
        xp_scr[CONV_BASE - CONV_HIST:CONV_BASE, :] = jnp.zeros((CONV_HIST, SSM_CONV_DIM), F32)
        hout_ref[...] = jnp.zeros(hout_ref.shape, F32)

    x = x_ref[...]
    xn = _rms(x, gmix_ref[...]).astype(BF16)
    xbc = jnp.dot(xn, wx_ref[...], preferred_element_type=F32)
    cout_ref[...] = xbc[rows - CONV_HIST:rows, :]
    xp_scr[CONV_BASE:CONV_BASE + rows, :] = _bf16_round(xbc)
    z = jnp.dot(xn, wz_ref[...], preferred_element_type=F32)
    dt_raw = jnp.dot(xn, wdt_ref[...], preferred_element_type=F32)
    u, v = _gm_uv(xn, wu_ref, wv_ref, lng_ref, lnb_ref)

    tril = _tril(lc)
    for sub in range(rows // lc):
        r0 = sub * lc
        xc = _conv_silu(xp_scr, lc, cw_ref, cb_ref, CONV_BASE + r0)
        dt, acs = _decay_cumsum(dt_raw[r0:r0 + lc], dtb_ref, alog_ref, tril, lc)
        _ssd_wide(xc, dt, acs, tril, dsk_ref, ex_ref, ecol_ref, hout_ref, y_scr.at[pl.ds(r0, lc)], acst_scr.at[sub])
        _spatial_gate(u[r0:r0 + lc], v[r0:r0 + lc], tril, ws_ref, bst_ref, sg_scr.at[pl.ds(r0, lc)])
    xp_scr[CONV_BASE - CONV_HIST:CONV_BASE, :] = xp_scr[CONV_BASE + rows - CONV_HIST:CONV_BASE + rows, :]

    _gate_norm(y_scr[...], z, nw_ref, yn_scr)
    x1 = _gated_merge(x, xn, yn_scr[...], sg_scr[...], wg_ref, bg_ref, wa_ref, wb_ref, wo_ref)

    q = _bdot(_rms(x1, gmem_ref[...]), wq_ref[...]).astype(BF16)
    scale = MEM_HEAD_DIM ** -0.5
    for h in range(MEM_HEADS):
        lo = h * MEM_HEAD_DIM
        s = lax.dot_general(q[:, lo:lo + MEM_HEAD_DIM], kb_ref[h], NT_DIMS, preferred_element_type=F32) * scale
        o_scr[:, lo:lo + MEM_HEAD_DIM] = jnp.dot(_softmax_rows(s).astype(BF16), vb_ref[h],
                                                 preferred_element_type=F32).astype(BF16)
    x2_ref[...] = x1 + _bdot(o_scr[...], wmo_ref[...])


def _mixer_weight_specs(w, names):
    return [_full(w[n].shape) for n in names], [w[n] for n in names]


def _prompt_block(x, kb, vb, layer, w, bn, seq):
    lc = PROMPT_ROWS if seq % PROMPT_ROWS == 0 else PROMPT_CHUNK
    hp = SSM_HEADS * SSM_HEAD_DIM
    m = kb.shape[3]
    blk = pl.BlockSpec((None, lc, D_MODEL), lambda b, c: (b, c, 0))
    kv_spec = pl.BlockSpec((None, None, MEM_HEADS, m, MEM_HEAD_DIM), lambda b, c: (layer, b, 0, 0, 0))
    once = lambda shape: pl.BlockSpec(shape, lambda b, c: (0,) * len(shape), pipeline_mode=pl.Buffered(1))
    x2, h_new, conv_new = pl.pallas_call(
        _prompt_block_body,
        grid=(bn, seq // lc),
        in_specs=[blk, kv_spec, kv_spec] + [once(w[n].shape) for n in PROMPT_BLOCK_WEIGHTS],
        out_specs=[blk,
                   pl.BlockSpec((None, hp, SSM_STATE), lambda b, c: (b, 0, 0)),
                   pl.BlockSpec((None, CONV_HIST, SSM_CONV_DIM), lambda b, c: (b, 0, 0))],
        out_shape=[jax.ShapeDtypeStruct((bn, seq, D_MODEL), F32),
                   jax.ShapeDtypeStruct((bn, hp, SSM_STATE), F32),
                   jax.ShapeDtypeStruct((bn, CONV_HIST, SSM_CONV_DIM), F32)],
        scratch_shapes=[pltpu.VMEM((CONV_BASE + lc, SSM_CONV_DIM), F32),
                        pltpu.VMEM((lc, SSM_D_INNER), F32),
                        pltpu.VMEM((lc // PROMPT_CHUNK, LANES, PROMPT_CHUNK), F32),
                        pltpu.VMEM((lc, SSM_D_INNER), BF16),
                        pltpu.VMEM((lc, GM_WIDTH), BF16),
                        pltpu.VMEM((lc, D_MODEL), BF16)],
        compiler_params=pltpu.CompilerParams(dimension_semantics=("arbitrary", "arbitrary"),
                                             vmem_limit_bytes=PROMPT_BLOCK_VMEM_LIMIT),
        name="prompt_block",
    )(x.reshape(bn, seq, D_MODEL), kb, vb, *[w[n] for n in PROMPT_BLOCK_WEIGHTS])
    return x2.reshape(bn * seq, D_MODEL), h_new, conv_new


def _ssd_short(xc, dt, acs, lv, dsk_ref, ex_ref, h_ref, hout_ref):
    lc = xc.shape[0]
    tot = acs[lc - 1:lc, :]
    e_acs = jnp.exp(acs)
    dt_end = dt * jnp.exp(tot - acs)
    c_col = jnp.exp(acs.T[:, lc - 1:lc])
    row = lax.broadcasted_iota(jnp.int32, (lc, LANES), 0)
    head = lax.broadcasted_iota(jnp.int32, (lc, LANES), 1)
    bms, cms, cbs = [], [], []
    for g in range(SSM_GROUPS):
        b0 = SSM_D_INNER + g * SSM_STATE
        c0 = SSM_D_INNER + SSM_GROUPS * SSM_STATE + g * SSM_STATE
        bms.append(xc[:, b0:b0 + SSM_STATE].astype(BF16))
        cms.append(xc[:, c0:c0 + SSM_STATE].astype(BF16))
        cbs.append(lax.dot_general(cms[g], bms[g], NT_DIMS, preferred_element_type=F32))
    coef = []
    for s in range(lv):
        decay = jnp.where(row >= s, jnp.exp(acs - acs[s:s + 1, :]), 0.0) * dt[s:s + 1, :]
        cb_s = jnp.zeros((lc, LANES), F32)
        for g in range(SSM_GROUPS):
            in_group = (head >= g * SSM_HEADS_PER_GROUP) & (head < (g + 1) * SSM_HEADS_PER_GROUP)
            cb_s = jnp.where(in_group, cbs[g][:, s:s + 1], cb_s)
        coef.append(decay * cb_s)
    wide = jnp.dot(_split_bf16(jnp.concatenate(coef + [e_acs, dt_end], axis=0)), ex_ref[...],
                   preferred_element_type=F32)
    xs = xc[:, :SSM_D_INNER]
    y = dsk_ref[...] * xs
    for s in range(lv):
        y = y + wide[s * lc:(s + 1) * lc] * xs[s:s + 1, :]
    e_acsx = wide[lv * lc:(lv + 1) * lc]
    xdte_b = (xs * wide[(lv + 1) * lc:]).astype(BF16)
    y_off = []
    for g in range(SSM_GROUPS):
        g0 = g * SSM_GROUP_WIDTH
        h_grp = h_ref[g0:g0 + SSM_GROUP_WIDTH, :]
        y_off.append(lax.dot_general(cms[g], h_grp.astype(BF16), NT_DIMS, preferred_element_type=F32))
        st = lax.dot_general(xdte_b[:, g0:g0 + SSM_GROUP_WIDTH], bms[g], TN_DIMS, preferred_element_type=F32)
        for r in range(SSM_HEADS_PER_GROUP):
            h = g * SSM_HEADS_PER_GROUP + r
            r0 = r * SSM_HEAD_DIM
            hout_ref[g0 + r0:g0 + r0 + SSM_HEAD_DIM, :] = (h_grp[r0:r0 + SSM_HEAD_DIM, :] * c_col[h:h + 1, :]
                                                           + st[r0:r0 + SSM_HEAD_DIM, :])
    return y + jnp.concatenate(y_off, axis=1) * e_acsx


def _mixer_sample_body(*refs, lv, sb, stack_prev):
    xbc_ref, dt_ref, z_ref, u_ref, v_ref, cprev_ref, h0_ref = refs[:7]
    i = 7
    if stack_prev:
        hprev_ref = refs[i]
        i += 1
    cw_ref, cb_ref, dtb_ref, alog_ref, dsk_ref, nw_ref, ws_ref, bst_ref, ex_ref = refs[i:i + 9]
    y_ref, sg_ref, hout_ref, cout_ref, xp_scr = refs[i + 9:]
    lc = SAMPLE_ROWS
    tril = _tril(lc)
    for s in range(sb):
        win = xp_scr.at[s]
        xbc = xbc_ref[s]
        win[CONV_BASE - CONV_HIST:CONV_BASE, :] = _bf16_round(cprev_ref[s])
        win[CONV_BASE:CONV_BASE + lc, :] = _bf16_round(xbc)
        xc = _conv_silu(win, lc, cw_ref, cb_ref)
        cout_ref[s] = xbc[lv - CONV_HIST:lv, :]
        dt, acs = _decay_cumsum(dt_ref[s], dtb_ref, alog_ref, tril, lv)
        if stack_prev:
            hout_ref[0, s] = hprev_ref[s]
            h_out = hout_ref.at[1, s]
        else:
            h_out = hout_ref.at[s]
        y = _ssd_short(xc, dt, acs, lv, dsk_ref, ex_ref, h0_ref.at[s], h_out)
        _gate_norm(y, z_ref[s], nw_ref, y_ref.at[s])
        _spatial_gate(u_ref[s], v_ref[s], tril, ws_ref, bst_ref, sg_ref.at[s])


def _mixer_sample(z, xbc, dt, u, v, w, bn, lv, state_ssm, state_conv, layer, h_prev_layer):
    lc = SAMPLE_ROWS
    sb = SAMPLE_SEQS_PER_STEP if bn % SAMPLE_SEQS_PER_STEP == 0 else 1
    stack_prev = h_prev_layer is not None
    hp = SSM_HEADS * SSM_HEAD_DIM
    r3 = lambda a: a.reshape(bn, lc, a.shape[-1])
    blk = lambda n: pl.BlockSpec((sb, lc, n), lambda b: (b, 0, 0))
    in_specs = [blk(SSM_CONV_DIM), blk(LANES), blk(SSM_D_INNER), blk(GM_WIDTH), blk(GM_WIDTH),
                pl.BlockSpec((None, sb, CONV_HIST, SSM_CONV_DIM), lambda b: (layer, b, 0, 0)),
                pl.BlockSpec((None, sb, hp, SSM_STATE), lambda b: (layer, b, 0, 0))]
    args = [r3(xbc), r3(dt), r3(z), r3(u), r3(v), state_conv, state_ssm]
    if stack_prev:
        in_specs.append(pl.BlockSpec((sb, hp, SSM_STATE), lambda b: (b, 0, 0)))
        args.append(h_prev_layer)
        h_spec = pl.BlockSpec((2, sb, hp, SSM_STATE), lambda b: (0, b, 0, 0))
        h_shape = jax.ShapeDtypeStruct((2, bn, hp, SSM_STATE), F32)
    else:
        h_spec = pl.BlockSpec((sb, hp, SSM_STATE), lambda b: (b, 0, 0))
        h_shape = jax.ShapeDtypeStruct((bn, hp, SSM_STATE), F32)
    w_specs, w_args = _mixer_weight_specs(w, ["conv_w", "conv_b", "dt_bias", "a_log", "d_skip", "ssm_norm", "gm_ws",
                                              "gm_bs_t", "expand_ch"])
    y, sg, h_new, conv_new = pl.pallas_call(
        functools.partial(_mixer_sample_body, lv=lv, sb=sb, stack_prev=stack_prev),
        grid=(bn // sb,),
        in_specs=in_specs + w_specs,
        out_specs=[blk(SSM_D_INNER), blk(GM_WIDTH), h_spec,
                   pl.BlockSpec((sb, CONV_HIST, SSM_CONV_DIM), lambda b: (b, 0, 0))],
        out_shape=[jax.ShapeDtypeStruct((bn, lc, SSM_D_INNER), F32), jax.ShapeDtypeStruct((bn, lc, GM_WIDTH), F32),
                   h_shape, jax.ShapeDtypeStruct((bn, CONV_HIST, SSM_CONV_DIM), F32)],
        scratch_shapes=[pltpu.VMEM((sb, CONV_BASE + lc, SSM_CONV_DIM), F32)],
        compiler_params=_params(1),
        name="mixer_sample",
    )(*args, *w_args)
    t = bn * lc
    return y.reshape(t, SSM_D_INNER), sg.reshape(t, GM_WIDTH), h_new, conv_new


def _gated_merge(x, xn, y, sg, wg_ref, bg_ref, wa_ref, wb_ref, wo_ref):
    gates = jax.nn.sigmoid(jnp.dot(xn, wg_ref[...], preferred_element_type=F32) + bg_ref[...])
    merged = gates[:, :D_MODEL] * _bdot(y, wa_ref[...]) + gates[:, D_MODEL:] * _bdot(sg, wb_ref[...])
    return x + _bdot(merged, wo_ref[...])


def _merge_body(x_ref, y_ref, sg_ref, gmix_ref, wg_ref, bg_ref, wa_ref, wb_ref, wo_ref, gmem_ref, wq_ref,
                x1_ref, q_ref):
    x = x_ref[...]
    xn = _rms(x, gmix_ref[...]).astype(BF16)
    x1 = _gated_merge(x, xn, y_ref[...], sg_ref[...], wg_ref, bg_ref, wa_ref, wb_ref, wo_ref)
    x1_ref[...] = x1
    q_ref[...] = _bdot(_rms(x1, gmem_ref[...]), wq_ref[...]).astype(BF16)


def _merge(x, y, sg, w, tm):
    t = x.shape[0]
    row = lambda n: pl.BlockSpec((tm, n), lambda i: (i, 0))
    return pl.pallas_call(
        _merge_body,
        grid=(t // tm,),
        in_specs=[row(D_MODEL), row(SSM_D_INNER), row(GM_WIDTH), _full((1, D_MODEL)),
                  _full((D_MODEL, 2 * D_MODEL)), _full((1, 2 * D_MODEL)), _full((SSM_D_INNER, D_MODEL)),
                  _full((GM_WIDTH, D_MODEL)), _full((D_MODEL, D_MODEL)), _full((1, D_MODEL)), _full((D_MODEL, D_MODEL))],
        out_specs=[row(D_MODEL), row(D_MODEL)],
        out_shape=[jax.ShapeDtypeStruct((t, D_MODEL), F32), jax.ShapeDtypeStruct((t, D_MODEL), BF16)],
        compiler_params=_params(1),
        name="merge",
    )(x, y, sg, w["norm_mix"], w["w_gate"], w["b_gate"], w["w_a_proj"], w["w_b_proj"], w["w_out"],
      w["norm_mem"], w["w_mem_q"])


def _memkv_body(m_ref, g_ref, wk_ref, wv_ref, k_ref, v_ref, kb_ref, vb_ref):
    mn = _rms(m_ref[...], g_ref[...]).astype(BF16)
    k = jnp.dot(mn, wk_ref[...], preferred_element_type=F32)
    v = jnp.dot(mn, wv_ref[...], preferred_element_type=F32)
    for h in range(MEM_HEADS):
        lo = h * MEM_HEAD_DIM
        k_ref[:, h, :] = k[:, lo:lo + MEM_HEAD_DIM]
        v_ref[:, h, :] = v[:, lo:lo + MEM_HEAD_DIM]
        kb_ref[h] = k[:, lo:lo + MEM_HEAD_DIM].astype(BF16)
        vb_ref[h] = v[:, lo:lo + MEM_HEAD_DIM].astype(BF16)


def _memkv(mem, norm_memkv, w_mem_k, w_mem_v):
    depth = w_mem_k.shape[0]
    bn, m, _ = mem.shape
    kv_spec = pl.BlockSpec((None, None, m, MEM_HEADS, MEM_HEAD_DIM), lambda l, b: (l, b, 0, 0, 0))
    kv_shape = jax.ShapeDtypeStruct((depth, bn, m, MEM_HEADS, MEM_HEAD_DIM), F32)
    hb_spec = pl.BlockSpec((None, None, MEM_HEADS, m, MEM_HEAD_DIM), lambda l, b: (l, b, 0, 0, 0))
    hb_shape = jax.ShapeDtypeStruct((depth, bn, MEM_HEADS, m, MEM_HEAD_DIM), BF16)
    w_spec = pl.BlockSpec((None, D_MODEL, D_MODEL), lambda l, b: (l, 0, 0))
    return pl.pallas_call(
        _memkv_body,
        grid=(depth, bn),
        in_specs=[pl.BlockSpec((None, m, D_MODEL), lambda l, b: (b, 0, 0)),
                  pl.BlockSpec((None, 1, D_MODEL), lambda l, b: (l, 0, 0)), w_spec, w_spec],
        out_specs=[kv_spec, kv_spec, hb_spec, hb_spec],
        out_shape=[kv_shape, kv_shape, hb_shape, hb_shape],
        compiler_params=_params(2),
        name="memkv",
    )(mem, norm_memkv.reshape(depth, 1, D_MODEL), w_mem_k.astype(BF16), w_mem_v.astype(BF16))


def _softmax_rows(s):
    e = jnp.exp(s - jnp.max(s, axis=-1, keepdims=True))
    return e / jnp.sum(e, axis=-1, keepdims=True)


def _attn_sample_body(q_ref, k_ref, v_ref, x_ref, wo_ref, out_ref, o_scr):
    sb, rows, _ = q_ref.shape
    m = k_ref.shape[1]
    shape = (MEM_HEADS * rows, m * MEM_HEADS)
    own_head = (lax.broadcasted_iota(jnp.int32, shape, 0) // rows
                == lax.broadcasted_iota(jnp.int32, shape, 1) % MEM_HEADS)
    for i in range(sb):
        k2 = k_ref[i].reshape(m * MEM_HEADS, MEM_HEAD_DIM).astype(BF16)
        v2 = v_ref[i].reshape(m * MEM_HEADS, MEM_HEAD_DIM).astype(BF16)
        q = q_ref[i].astype(F32)
        q4 = jnp.concatenate([q[:, h * MEM_HEAD_DIM:(h + 1) * MEM_HEAD_DIM] for h in range(MEM_HEADS)], axis=0)
        s = lax.dot_general(q4.astype(BF16), k2, NT_DIMS, preferred_element_type=F32) * (MEM_HEAD_DIM ** -0.5)
        p = _softmax_rows(jnp.where(own_head, s, -jnp.inf))
        o4 = jnp.dot(p.astype(BF16), v2, preferred_element_type=F32)
        for h in range(MEM_HEADS):
            o_scr[i * rows:(i + 1) * rows, h * MEM_HEAD_DIM:(h + 1) * MEM_HEAD_DIM] = o4[h * rows:(h + 1) * rows, :]
    proj = _bdot(o_scr[...], wo_ref[...])
    for i in range(sb):
        out_ref[i] = x_ref[i] + proj[i * rows:(i + 1) * rows, :]


def _attn_sample(q, k5, v5, layer, x1, w, bn, rows):
    m = k5.shape[2]
    sb = ATTN_SEQS_PER_STEP if bn % ATTN_SEQS_PER_STEP == 0 else 1
    r3 = lambda a: a.reshape(bn, rows, D_MODEL)
    blk = pl.BlockSpec((sb, rows, D_MODEL), lambda b: (b, 0, 0))
    kv_spec = pl.BlockSpec((None, sb, m, MEM_HEADS, MEM_HEAD_DIM), lambda b: (layer, b, 0, 0, 0))
    out = pl.pallas_call(
        _attn_sample_body,
        grid=(bn // sb,),
        in_specs=[blk, kv_spec, kv_spec, blk, _full((D_MODEL, D_MODEL))],
        out_specs=blk,
        out_shape=jax.ShapeDtypeStruct((bn, rows, D_MODEL), F32),
        scratch_shapes=[pltpu.VMEM((sb * rows, D_MODEL), F32)],
        compiler_params=_params(1),
        name="attn_sample",
    )(r3(q), k5, v5, r3(x1), w["w_mem_o"])
    return out.reshape(bn * rows, D_MODEL)


PACKED_WIDTH = D_MODEL // 2
HIGH_HALF = 0xFFFF0000


def _pack_bf16_pairs(x):
    bits = lax.bitcast_convert_type(_bf16_round(x), jnp.uint32)
    return (bits[:, :PACKED_WIDTH] >> 16) | (bits[:, PACKED_WIDTH:] & jnp.uint32(HIGH_HALF))


def _unpack_bf16_pairs(p):
    lo = lax.bitcast_convert_type(p << 16, F32)
    hi = lax.bitcast_convert_type(p & jnp.uint32(HIGH_HALF), F32)
    return jnp.concatenate([lo, hi], axis=1).astype(BF16)


def _first_max(vals):
    m = functools.reduce(jnp.maximum, vals)
    idx = jnp.full(m.shape, len(vals) - 1, jnp.int32)
    for j in range(len(vals) - 2, -1, -1):
        idx = jnp.where(vals[j] == m, j, idx)
    return m, idx


def _route_body(x_ref, g_ref, wr_ref, br_ref, xn_ref, eid_ref, rank_ref, wt_ref, cnt_ref, carry_scr, *, tm):
    @pl.when(pl.program_id(0) == 0)
    def _():
        carry_scr[...] = jnp.zeros(carry_scr.shape, F32)

    xn = _rms(x_ref[...], g_ref[...])
    xn_ref[...] = _pack_bf16_pairs(xn)
    lg = lax.dot_general(wr_ref[...].astype(BF16), xn.astype(BF16), NT_DIMS, preferred_element_type=F32) + br_ref[...]
    grp = [lg[j:j + 1, :] for j in range(MOE_GROUPS)]
    gmax, gid = _first_max(grp)
    gw = 1.0 / functools.reduce(jnp.add, [jnp.exp(r - gmax) for r in grp])
    ex = [lg[SUBLANES + j:SUBLANES + j + 1, :] for j in range(MOE_EXPERTS)]
    sel = []
    for j in range(MOE_EXPERTS_PER_GROUP):
        pick = ex[(MOE_GROUPS - 1) * MOE_EXPERTS_PER_GROUP + j]
        for g in range(MOE_GROUPS - 2, -1, -1):
            pick = jnp.where(gid == g, ex[g * MOE_EXPERTS_PER_GROUP + j], pick)
        sel.append(pick)
    v1, i1 = _first_max(sel)
    rest = [jnp.where(i1 == j, -jnp.inf, sel[j]) for j in range(MOE_EXPERTS_PER_GROUP)]
    v2, i2 = _first_max(rest)
    e2 = jnp.exp(v2 - v1)
    den = 1.0 + e2
    eid1 = gid * MOE_EXPERTS_PER_GROUP + i1
    eid2 = gid * MOE_EXPERTS_PER_GROUP + i2

    e_iota = lax.broadcasted_iota(jnp.int32, (MOE_EXPERTS, tm), 0)
    m1 = e_iota == eid1
    m2 = e_iota == eid2
    onehot = jnp.where(m1, 1.0, 0.0) + jnp.where(m2, 1.0, 0.0)
    before = lax.broadcasted_iota(jnp.int32, (tm, tm), 0) < lax.broadcasted_iota(jnp.int32, (tm, tm), 1)
    ranks = _bdot(onehot, jnp.where(before, 1.0, 0.0)) + carry_scr[:, 0:1]
    r1 = jnp.sum(jnp.where(m1, ranks, 0.0), axis=0, keepdims=True)
    r2 = jnp.sum(jnp.where(m2, ranks, 0.0), axis=0, keepdims=True)
    carry_scr[...] = carry_scr[...] + jnp.sum(onehot, axis=1, keepdims=True)
    cnt_ref[...] = carry_scr[...]

    zeros_i = jnp.zeros((SUBLANES - MOE_TOP_K, tm), jnp.int32)
    eid_ref[0:1, :] = eid1
    eid_ref[1:2, :] = eid2
    eid_ref[MOE_TOP_K:, :] = zeros_i
    rank_ref[0:1, :] = r1.astype(jnp.int32)
    rank_ref[1:2, :] = r2.astype(jnp.int32)
    rank_ref[MOE_TOP_K:, :] = zeros_i
    wt_ref[0:1, :] = gw / den
    wt_ref[1:2, :] = gw * e2 / den
    wt_ref[MOE_TOP_K:, :] = jnp.zeros((SUBLANES - MOE_TOP_K, tm), F32)


def _route(x, w, tm):
    t = x.shape[0]
    lane_blk = pl.BlockSpec((SUBLANES, tm), lambda i: (0, i))
    nr = w["w_router_t"].shape[0]
    return pl.pallas_call(
        functools.partial(_route_body, tm=tm),
        grid=(t // tm,),
        in_specs=[pl.BlockSpec((tm, D_MODEL), lambda i: (i, 0)), _full((1, D_MODEL)), _full((nr, D_MODEL)), _full((nr, 1))],
        out_specs=[pl.BlockSpec((tm, PACKED_WIDTH), lambda i: (i, 0)), lane_blk, lane_blk, lane_blk,
                   _full((MOE_EXPERTS, LANES))],
        out_shape=[jax.ShapeDtypeStruct((t, PACKED_WIDTH), jnp.uint32), jax.ShapeDtypeStruct((SUBLANES, t), jnp.int32),
                   jax.ShapeDtypeStruct((SUBLANES, t), jnp.int32), jax.ShapeDtypeStruct((SUBLANES, t), F32),
                   jax.ShapeDtypeStruct((MOE_EXPERTS, LANES), F32)],
        scratch_shapes=[pltpu.VMEM((MOE_EXPERTS, LANES), F32)],
        compiler_params=_params(1),
        name="route",
    )(x, w["norm_ffn"], w["w_router_t"], w["b_router_t"])


def _row_copy(src, src_row, dst, dst_row, sem):
    return pltpu.make_async_copy(src.at[pl.ds(src_row, 1), :], dst.at[pl.ds(dst_row, 1), :], sem)


META_ENDS = 0
META_SIZES = MOE_EXPERTS
META_USED = 2 * MOE_EXPERTS


def _dispatch_body(meta_ref, dest_ref, prev_dest_ref, xn_ref, out_ref, zbuf, stage, sem, zsem, *, tm, rows, n_blocks):
    i = pl.program_id(0)
    slot = i % 2

    @pl.when(i == 0)
    def _():
        zbuf[...] = jnp.zeros(zbuf.shape, zbuf.dtype)

        def zero_fills(act):
            for e in range(MOE_EXPERTS):
                @pl.when(meta_ref[META_SIZES + e] > 0)
                def _(e=e):
                    lo = pl.multiple_of(meta_ref[META_ENDS + e] - rows, rows)
                    act(pltpu.make_async_copy(zbuf, out_ref.at[pl.ds(lo, rows), :], zsem))
            for j in range(MOE_EXPERTS):
                @pl.when(meta_ref[META_USED] + j < n_blocks)
                def _(j=j):
                    lo = pl.multiple_of((meta_ref[META_USED] + j) * rows, rows)
                    act(pltpu.make_async_copy(zbuf, out_ref.at[pl.ds(lo, rows), :], zsem))

        zero_fills(lambda cp: cp.start())
        zero_fills(lambda cp: cp.wait())

    def copies(idx_ref, s, act):
        def body(t, carry):
            for k in range(MOE_TOP_K):
                act(_row_copy(stage.at[s], t, out_ref, idx_ref[k, t], sem.at[s]))
            return carry
        lax.fori_loop(0, tm, body, 0, unroll=DMA_UNROLL)

    stage[slot] = xn_ref[...]
    copies(dest_ref, slot, lambda cp: cp.start())

    @pl.when(i > 0)
    def _():
        copies(prev_dest_ref, 1 - slot, lambda cp: cp.wait())

    @pl.when(i == pl.num_programs(0) - 1)
    def _():
        copies(dest_ref, slot, lambda cp: cp.wait())


def _dispatch(xn, dest, meta, n_blocks, rows, tm):
    t = xn.shape[0]
    grid_spec = pltpu.PrefetchScalarGridSpec(
        num_scalar_prefetch=1,
        grid=(t // tm,),
        in_specs=[pl.BlockSpec((MOE_TOP_K, tm), lambda i, meta: (0, i), memory_space=pltpu.SMEM),
                  pl.BlockSpec((MOE_TOP_K, tm), lambda i, meta: (0, jnp.maximum(i - 1, 0)), memory_space=pltpu.SMEM),
                  pl.BlockSpec((tm, PACKED_WIDTH), lambda i, meta: (i, 0))],
        out_specs=pl.BlockSpec(memory_space=pl.ANY),
        scratch_shapes=[pltpu.VMEM((rows, PACKED_WIDTH), jnp.uint32), pltpu.VMEM((2, tm, PACKED_WIDTH), jnp.uint32),
                        pltpu.SemaphoreType.DMA((2,)), pltpu.SemaphoreType.DMA(())],
    )
    return pl.pallas_call(
        functools.partial(_dispatch_body, tm=tm, rows=rows, n_blocks=n_blocks),
        grid_spec=grid_spec,
        out_shape=jax.ShapeDtypeStruct((n_blocks * rows, PACKED_WIDTH), jnp.uint32),
        compiler_params=_params(1, disable_bounds_checks=True),
        name="dispatch",
    )(meta, dest, dest, xn)


def _ffn_body(be_ref, meta_ref, x_ref, wg_ref, wu_ref, wd_ref, o_ref, wg_scr, wu_scr, wd_scr):
    i = pl.program_id(0)
    last = jnp.maximum(meta_ref[META_USED] - 1, 0)
    used = i < meta_ref[META_USED]
    e_now = be_ref[jnp.minimum(i, last)]
    e_before = be_ref[jnp.minimum(jnp.maximum(i - 1, 0), last)]

    @pl.when(jnp.logical_or(i == 0, e_now != e_before))
    def _():
        wg_scr[...] = wg_ref[...].astype(BF16)
        wu_scr[...] = wu_ref[...].astype(BF16)
        wd_scr[...] = wd_ref[...].astype(BF16)

    @pl.when(used)
    def _():
        x = _unpack_bf16_pairs(x_ref[...])
        hid = (_silu(jnp.dot(x, wg_scr[...], preferred_element_type=F32))
               * jnp.dot(x, wu_scr[...], preferred_element_type=F32))
        o_ref[...] = jnp.dot(hid.astype(BF16), wd_scr[...], preferred_element_type=F32)

    @pl.when(jnp.logical_not(used))
    def _():
        o_ref[...] = jnp.zeros(o_ref.shape, F32)


def _ffn(xs, block_e, meta, w, layer, rows):
    n_slots = xs.shape[0]
    last_used = lambda i, meta: jnp.minimum(i, jnp.maximum(meta[META_USED] - 1, 0))
    blk_i = lambda i, be, meta: (last_used(i, meta), 0)
    exp_i = lambda i, be, meta: (layer, be[last_used(i, meta)], 0, 0)
    grid_spec = pltpu.PrefetchScalarGridSpec(
        num_scalar_prefetch=2,
        grid=(n_slots // rows,),
        in_specs=[pl.BlockSpec((rows, PACKED_WIDTH), blk_i),
                  pl.BlockSpec((None, None, D_MODEL, MOE_D_FF), exp_i),
                  pl.BlockSpec((None, None, D_MODEL, MOE_D_FF), exp_i),
                  pl.BlockSpec((None, None, MOE_D_FF, D_MODEL), exp_i)],
        out_specs=pl.BlockSpec((rows, D_MODEL), lambda i, be, meta: (i, 0)),
        scratch_shapes=[pltpu.VMEM((D_MODEL, MOE_D_FF), BF16), pltpu.VMEM((D_MODEL, MOE_D_FF), BF16),
                        pltpu.VMEM((MOE_D_FF, D_MODEL), BF16)],
    )
    return pl.pallas_call(
        _ffn_body,
        grid_spec=grid_spec,
        out_shape=jax.ShapeDtypeStruct((n_slots, D_MODEL), F32),
        compiler_params=_params(1),
        name="ffn",
    )(block_e, meta, xs, w["w_expert_gate"], w["w_expert_up"], w["w_expert_down"])


def _combine_body(dest_ref, prev_dest_ref, wt_ref, x_ref, ys_ref, gfin_ref, out_ref, bufs, sem, *, tm, final):
    i = pl.program_id(0)
    n_tiles = pl.num_programs(0) - 1
    slot = i % 2

    def copies(idx_ref, s, act):
        def body(t, carry):
            for k in range(MOE_TOP_K):
                act(_row_copy(ys_ref, idx_ref[k, t], bufs.at[s, k], t, sem.at[s]))
            return carry
        lax.fori_loop(0, tm, body, 0, unroll=DMA_UNROLL)

    @pl.when(i < n_tiles)
    def _():
        copies(dest_ref, slot, lambda cp: cp.start())

    @pl.when(i > 0)
    def _():
        copies(prev_dest_ref, 1 - slot, lambda cp: cp.wait())
        wt = wt_ref[...].T
        out = x_ref[...] + (wt[:, 0:1] * bufs[1 - slot, 0] + wt[:, 1:2] * bufs[1 - slot, 1])
        if final:
            out = _rms(out, gfin_ref[...])
        out_ref[...] = out


def _combine(x, ys, dest, wt, g_final, tm, final):
    t = x.shape[0]
    n_tiles = t // tm
    cur = lambda i: jnp.minimum(i, n_tiles - 1)
    prev = lambda i: jnp.maximum(i - 1, 0)
    return pl.pallas_call(
        functools.partial(_combine_body, tm=tm, final=final),
        grid=(n_tiles + 1,),
        in_specs=[pl.BlockSpec((MOE_TOP_K, tm), lambda i: (0, cur(i)), memory_space=pltpu.SMEM),
                  pl.BlockSpec((MOE_TOP_K, tm), lambda i: (0, prev(i)), memory_space=pltpu.SMEM),
                  pl.BlockSpec((SUBLANES, tm), lambda i: (0, prev(i))),
                  pl.BlockSpec((tm, D_MODEL), lambda i: (prev(i), 0)),
                  pl.BlockSpec(memory_space=pl.ANY),
                  _full((1, D_MODEL))],
        out_specs=pl.BlockSpec((tm, D_MODEL), lambda i: (prev(i), 0)),
        out_shape=jax.ShapeDtypeStruct((t, D_MODEL), F32),
        scratch_shapes=[pltpu.VMEM((2, MOE_TOP_K, tm, D_MODEL), F32), pltpu.SemaphoreType.DMA((2,))],
        compiler_params=_params(1, disable_bounds_checks=True),
        name="combine",
    )(dest, dest, wt, x, ys, g_final)


def _moe(x, w, layer, g_final, final, tm, rows):
    t = x.shape[0]
    xn, eid, rank, wt, cnt = _route(x, w, tm)
    counts = cnt[:, 0].astype(jnp.int32)
    padded = (counts + rows - 1) // rows * rows
    pad_ends = jnp.cumsum(padded)
    pad_starts = pad_ends - padded
    n_blocks = (t * MOE_TOP_K) // rows + MOE_EXPERTS
    e_ids = jnp.arange(MOE_EXPERTS, dtype=jnp.int32)
    start_of = jnp.sum(jnp.where(eid[:MOE_TOP_K, :, None] == e_ids, pad_starts, 0), axis=-1)
    dest = start_of + rank[:MOE_TOP_K]
    block_lo = jnp.arange(n_blocks, dtype=jnp.int32) * rows
    block_e = jnp.minimum(jnp.sum((pad_ends[None, :] <= block_lo[:, None]).astype(jnp.int32), axis=1), MOE_EXPERTS - 1)
    meta = jnp.concatenate([pad_ends, padded, pad_ends[-1:] // rows]).astype(jnp.int32)
    xs = _dispatch(xn, dest, meta, n_blocks, rows, tm)
    ys = _ffn(xs, block_e, meta, w, layer, rows)
    return _combine(x, ys, dest, wt, g_final, tm, final)


def _expansion_tables():
    k = jnp.arange(SPLIT_PARTS * LANES, dtype=jnp.int32)[:, None] % LANES
    ch = jnp.arange(SSM_D_INNER, dtype=jnp.int32)[None, :] // SSM_HEAD_DIM
    col = jnp.arange(SSM_HEADS * LANES, dtype=jnp.int32)[None, :] // LANES
    return (k == ch).astype(BF16), (k == col).astype(BF16)


def _layer_weights(l, p):
    row = lambda a: a[l].reshape(1, -1)
    s1 = SSM_D_INNER
    s2 = s1 + SSM_CONV_DIM
    s3 = s2 + SSM_HEADS
    w_in = p["w_in"][l]
    pad_h = LANES - SSM_HEADS
    wr = jnp.zeros((SUBLANES + MOE_EXPERTS, D_MODEL), F32)
    wr = wr.at[:MOE_GROUPS].set(p["w_router_group"][l].T).at[SUBLANES:].set(p["w_router_expert"][l].T)
    br = jnp.zeros((SUBLANES + MOE_EXPERTS, 1), F32)
    br = br.at[:MOE_GROUPS, 0].set(p["b_router_group"][l]).at[SUBLANES:, 0].set(p["b_router_expert"][l].reshape(-1))
    expand_ch, expand_col = _expansion_tables()
    return {
        "norm_mix": row(p["norm_mix"]),
        "w_z": w_in[:, :s1].astype(BF16),
        "w_xbc": w_in[:, s1:s2].astype(BF16),
        "w_dt": jnp.pad(w_in[:, s2:s3], ((0, 0), (0, pad_h))).astype(BF16),
        "w_u": w_in[:, s3:s3 + GM_WIDTH].astype(BF16),
        "w_v": w_in[:, s3 + GM_WIDTH:].astype(BF16),
        "gm_ln_g": row(p["gm_ln_g"]), "gm_ln_b": row(p["gm_ln_b"]),
        "conv_w": p["conv_w"][l], "conv_b": row(p["conv_b"]),
        "dt_bias": jnp.pad(row(p["dt_bias"]), ((0, 0), (0, pad_h))),
        "a_log": jnp.pad(row(p["a_log"]), ((0, 0), (0, pad_h))),
        "d_skip": jnp.repeat(p["d_skip"][l], SSM_HEAD_DIM).reshape(1, -1),
        "ssm_norm": row(p["ssm_norm"]),
        "gm_ws": p["gm_ws"][l], "gm_bs_t": p["gm_bs"][l].T,
        "expand_ch": expand_ch, "expand_col": expand_col,
        "w_gate": p["w_gate"][l].astype(BF16), "b_gate": row(p["b_gate"]),
        "w_a_proj": p["w_a_proj"][l].astype(BF16), "w_b_proj": p["w_b_proj"][l].astype(BF16),
        "w_out": p["w_out"][l].astype(BF16),
        "norm_mem": row(p["norm_mem"]),
        "w_mem_q": p["w_mem_q"][l].astype(BF16), "w_mem_o": p["w_mem_o"][l].astype(BF16),
        "norm_ffn": row(p["norm_ffn"]),
        "w_router_t": wr, "b_router_t": br,
        "w_expert_gate": p["w_expert_gate"], "w_expert_up": p["w_expert_up"], "w_expert_down": p["w_expert_down"],
    }


def kernel(x_prompt, x_sample, mem_prompt, state_ssm, state_conv, cache_mem_k, cache_mem_v, norm_mix, w_in, conv_w, conv_b, dt_bias, a_log, d_skip, ssm_norm, w_a_proj, gm_ln_g, gm_ln_b, gm_ws, gm_bs, w_b_proj, w_gate, b_gate, w_out, norm_mem, norm_memkv, w_mem_q, w_mem_k, w_mem_v, w_mem_o, norm_ffn, w_router_group, b_router_group, w_router_expert, b_router_expert, w_expert_gate, w_expert_up, w_expert_down, norm_final):
    p = dict(norm_mix=norm_mix, w_in=w_in, conv_w=conv_w, conv_b=conv_b, dt_bias=dt_bias, a_log=a_log, d_skip=d_skip,
             ssm_norm=ssm_norm, w_a_proj=w_a_proj, gm_ln_g=gm_ln_g, gm_ln_b=gm_ln_b, gm_ws=gm_ws, gm_bs=gm_bs,
             w_b_proj=w_b_proj, w_gate=w_gate, b_gate=b_gate, w_out=w_out, norm_mem=norm_mem, norm_memkv=norm_memkv,
             w_mem_q=w_mem_q, w_mem_k=w_mem_k, w_mem_v=w_mem_v, w_mem_o=w_mem_o, norm_ffn=norm_ffn,
             w_router_group=w_router_group, b_router_group=b_router_group, w_router_expert=w_router_expert,
             b_router_expert=b_router_expert, w_expert_gate=w_expert_gate, w_expert_up=w_expert_up,
             w_expert_down=w_expert_down)
    depth = w_in.shape[0]
    bp, lp, _ = x_prompt.shape
    bs, ls, _ = x_sample.shape
    assert lp % PROMPT_CHUNK == 0 and ls <= SAMPLE_ROWS
    g_final = norm_final.reshape(1, -1)
    hp = SSM_HEADS * SSM_HEAD_DIM
    state_shape = (SSM_HEADS, SSM_HEAD_DIM, SSM_STATE)
    state_ssm2 = state_ssm.reshape(depth, bs, hp, SSM_STATE)

    xp = x_prompt.reshape(bp * lp, D_MODEL)
    xs = jnp.pad(x_sample, ((0, 0), (0, SAMPLE_ROWS - ls), (0, 0))).reshape(bs * SAMPLE_ROWS, D_MODEL)
    tp, ts = bp * lp, bs * SAMPLE_ROWS
    tm_s = min(256, ts)
    tmm_p, tmm_s = min(512, tp), min(512, ts)

    mk_p, mv_p, kb, vb = _memkv(mem_prompt, norm_memkv, w_mem_k, w_mem_v)
    ssm_p, conv_p, ssm_s, conv_s, gv_s = [], [], [], [], []
    for l in range(depth):
        w = _layer_weights(l, p)
        final = l == depth - 1
        x2, h_new, c_new = _prompt_block(xp, kb, vb, l, w, bp, lp)
        xp = _moe(x2, w, l, g_final, final, tmm_p, MOE_ROWS_PROMPT)
        ssm_p.append(h_new.reshape((bp,) + state_shape))
        conv_p.append(c_new)
        z, xbc, dt, u, v = _inproj(xs, w, tm_s)
        stack_two = depth == 2 and l == 1
        y, sg, h_new, c_new = _mixer_sample(z, xbc, dt, u, v, w, bs, ls, state_ssm2, state_conv, l,
                                            ssm_s[0] if stack_two else None)
        x1, q = _merge(xs, y, sg, w, tm_s)
        x2 = _attn_sample(q, cache_mem_k, cache_mem_v, l, x1, w, bs, SAMPLE_ROWS)
        xs = _moe(x2, w, l, g_final, final, tmm_s, MOE_ROWS_SAMPLE)
        if stack_two:
            ssm_s = h_new.reshape((depth, bs) + state_shape)
        else:
            ssm_s.append(h_new)
        conv_s.append(c_new)
        gv_s.append(v.reshape(bs, SAMPLE_ROWS, GM_WIDTH)[:, :ls])
    if isinstance(ssm_s, list):
        ssm_s = jnp.stack(ssm_s).reshape((depth, bs) + state_shape)
    y_prompt = xp.reshape(bp, lp, D_MODEL)
    y_sample = xs.reshape(bs, SAMPLE_ROWS, D_MODEL)[:, :ls]
    return (y_prompt, y_sample, jnp.stack(ssm_p), jnp.stack(conv_p), mk_p, mv_p,
            ssm_s, jnp.stack(conv_s), jnp.stack(gv_s))
```
